```python
import math
import jax, jax.numpy as jnp
from jax import lax
import numpy as np

D_MODEL = 2048
BATCH = 2
SEQ = 4096
DEPTH = 4
DEC_BATCH = 8
DEC_SEQ = 8
PAST_LEN = 16384
PAGE_SIZE = 128

MIXER_ORDER = ('pool', 'attn', 'conv')
N_MIXERS = len(MIXER_ORDER)
EXPAND = 2
D_INNER = EXPAND * D_MODEL
POOL_WINDOWS = (2, 4, 8, 16)
N_POOL_GROUPS = len(POOL_WINDOWS)
POOL_GROUP = D_INNER // N_POOL_GROUPS
POOL_STATE = max(POOL_WINDOWS) - 1
ATT_HEAD_DIM = 64
ATT_HEADS = D_MODEL // (2 * ATT_HEAD_DIM)
ATT_VDIM = 2 * ATT_HEAD_DIM
ATT_QK_WIDTH = ATT_HEADS * 2 * ATT_HEAD_DIM
ATT_WIDTH = ATT_HEADS * ATT_VDIM
Q_BLOCK = 128
ATT_LAYER_INDEX = 1
LAMBDA_INIT = 0.8 - 0.6 * math.exp(-0.3 * ATT_LAYER_INDEX)
CONV_WIDTH = 31
CONV_STATE = CONV_WIDTH - 1
NORM_EPS = 1e-6
NEG_INF = -1e30
SPARE_PAGE_DIVISOR = 4

kernel_name = 'hybrid_pool_diffattn_conformer_step'


def rms_norm(x, g):
    xf = x.astype(jnp.float32)
    y = xf * lax.rsqrt(jnp.mean(xf * xf, axis=-1, keepdims=True) + NORM_EPS)
    return (y * g.astype(jnp.float32)).astype(x.dtype)


def layer_norm(x, g, b):
    xf = x.astype(jnp.float32)
    xc = xf - jnp.mean(xf, axis=-1, keepdims=True)
    y = xc * lax.rsqrt(jnp.mean(xc * xc, axis=-1, keepdims=True) + NORM_EPS)
    return (y * g.astype(jnp.float32) + b.astype(jnp.float32)).astype(x.dtype)


def alibi_slopes():
    return jnp.exp2(-8.0 * jnp.arange(1, ATT_HEADS + 1, dtype=jnp.float32) / ATT_HEADS)


def multiscale_pool(u_ext, n_new, w_grp, scale):
    b, l, _ = u_ext.shape
    uf = u_ext.astype(jnp.float32).reshape(b, l, N_POOL_GROUPS, POOL_GROUP)
    cs = jnp.cumsum(uf, axis=1)
    pos = jnp.arange(l, dtype=jnp.float32)[l - n_new:]
    outs = []
    for gi, w in enumerate(POOL_WINDOWS):
        cs_g = cs[:, :, gi]
        lagged = jnp.pad(cs_g, ((0, 0), (w, 0), (0, 0)))[:, :l]
        count = jnp.minimum(pos + 1.0, float(w))[None, :, None]
        mean_w = (cs_g - lagged)[:, l - n_new:] / count
        outs.append(mean_w - uf[:, l - n_new:, gi])
    pooled = jnp.stack(outs, axis=2)
    mixed = jnp.einsum('btgi,gij->btgj', pooled, w_grp.astype(jnp.float32))
    mixed = mixed.reshape(b, n_new, D_INNER) * scale.astype(jnp.float32)
    return mixed.astype(u_ext.dtype)


def pool_layer(x, carry, norm_g, w_in, w_grp, scale, w_out):
    t = x.shape[1]
    h = rms_norm(x, norm_g)
    u, gate = jnp.split(h @ w_in, 2, axis=-1)
    u_ext = u if carry is None else jnp.concatenate([carry.astype(u.dtype), u], axis=1)
    mix = multiscale_pool(u_ext, t, w_grp, scale)
    y = x + (mix * jax.nn.silu(gate)) @ w_out
    return y, u_ext[:, -POOL_STATE:]


def diff_lambda(lq1, lk1, lq2, lk2):
    f = jnp.float32
    return (jnp.exp(jnp.sum(lq1.astype(f) * lk1.astype(f)))
            - jnp.exp(jnp.sum(lq2.astype(f) * lk2.astype(f))) + LAMBDA_INIT)


def diff_attn_project(x, norm_g, w_in, q_norm_g, k_norm_g):
    b, t, _ = x.shape
    h = rms_norm(x, norm_g)
    q, k, v, gate = jnp.split(h @ w_in, [ATT_QK_WIDTH, 2 * ATT_QK_WIDTH, 2 * ATT_QK_WIDTH + ATT_WIDTH], axis=-1)
    q = rms_norm(q.reshape(b, t, ATT_HEADS, 2, ATT_HEAD_DIM), q_norm_g)
    k = rms_norm(k.reshape(b, t, ATT_HEADS, 2, ATT_HEAD_DIM), k_norm_g)
    v = v.reshape(b, t, ATT_HEADS, ATT_VDIM)
    return q, k, v, gate


def diff_attend(q, k, v, q_pos, k_pos, lam):
    s = jnp.einsum('bqhcd,bkhcd->bhcqk', q, k, preferred_element_type=jnp.float32) * (ATT_HEAD_DIM ** -0.5)
    dist = (q_pos[:, None] - k_pos[None, :]).astype(jnp.float32)
    s = s - (alibi_slopes()[:, None, None] * dist)[None, :, None]
    s = jnp.where((dist >= 0.0)[None, None, None], s, NEG_INF)
    p = jax.nn.softmax(s, axis=-1)
    a = p[:, :, 0] - lam * p[:, :, 1]
    return jnp.einsum('bhqk,bkhe->bqhe', a.astype(v.dtype), v, preferred_element_type=jnp.float32)


def diff_attn_output(x, o, gate, subln_g, w_out):
    b, t = o.shape[:2]
    o = rms_norm(o, subln_g) * (1.0 - LAMBDA_INIT)
    o = o.reshape(b, t, ATT_WIDTH).astype(x.dtype)
    return x + (o * jax.nn.silu(gate)) @ w_out


def diff_attn_prompt(x, lam, norm_g, w_in, q_norm_g, k_norm_g, subln_g, w_out):
    b, t, _ = x.shape
    q, k, v, gate = diff_attn_project(x, norm_g, w_in, q_norm_g, k_norm_g)
    n_blk = t // Q_BLOCK
    q_blocks = jnp.moveaxis(q.reshape(b, n_blk, Q_BLOCK, ATT_HEADS, 2, ATT_HEAD_DIM), 1, 0)
    starts = jnp.arange(n_blk, dtype=jnp.int32) * Q_BLOCK
    k_pos = jnp.arange(t, dtype=jnp.int32)

    def block(args):
        qb, start = args
        return diff_attend(qb, k, v, start + jnp.arange(Q_BLOCK, dtype=jnp.int32), k_pos, lam)

    o = lax.map(block, (q_blocks, starts))
    o = jnp.moveaxis(o, 0, 1).reshape(b, t, ATT_HEADS, ATT_VDIM)
    return diff_attn_output(x, o, gate, subln_g, w_out), k, v


def diff_attn_sample(x, cache_k, cache_v, page_table, lam, norm_g, w_in, q_norm_g, k_norm_g, subln_g, w_out):
    b, t, _ = x.shape
    q, k, v, gate = diff_attn_project(x, norm_g, w_in, q_norm_g, k_norm_g)
    k_past = cache_k[page_table].reshape(b, PAST_LEN, ATT_HEADS, 2, ATT_HEAD_DIM)
    v_past = cache_v[page_table].reshape(b, PAST_LEN, ATT_HEADS, ATT_VDIM)
    k_all = jnp.concatenate([k_past.astype(k.dtype), k], axis=1)
    v_all = jnp.concatenate([v_past.astype(v.dtype), v], axis=1)
    q_pos = PAST_LEN + jnp.arange(t, dtype=jnp.int32)
    k_pos = jnp.arange(PAST_LEN + t, dtype=jnp.int32)
    o = diff_attend(q, k_all, v_all, q_pos, k_pos, lam)
    return diff_attn_output(x, o, gate, subln_g, w_out), k, v


def conv_layer(x, carry, norm_g, w_in, conv_w, conv_b, ln_g, ln_b, w_out):
    b, t, _ = x.shape
    h = rms_norm(x, norm_g)
    a, g_glu, gate = jnp.split(h @ w_in, 3, axis=-1)
    c = a * jax.nn.sigmoid(g_glu)
    prev = jnp.zeros((b, CONV_STATE, D_INNER), c.dtype) if carry is None else carry.astype(c.dtype)
    c_ext = jnp.concatenate([prev, c], axis=1)
    y = lax.conv_general_dilated(c_ext, conv_w[:, None, :].astype(c.dtype), window_strides=(1,),
                                 padding='VALID', dimension_numbers=('NWC', 'WIO', 'NWC'),
                                 feature_group_count=D_INNER)
    y = jax.nn.silu(layer_norm(y + conv_b.astype(y.dtype), ln_g, ln_b))
    out = x + (y * jax.nn.silu(gate)) @ w_out
    return out, c_ext[:, -CONV_STATE:]


def setup_inputs(seed: int = 0) -> dict:
    key = jax.random.key(seed)
    keys = iter(jax.random.split(key, 64))
    f32 = jnp.float32

    def normal(shape, scale=1.0):
        return scale * jax.random.normal(next(keys), shape, f32)

    def gain(n):
        return 1.0 + 0.05 * normal((n,))

    n_pages = PAST_LEN // PAGE_SIZE
    n_used = DEC_BATCH * n_pages
    n_phys = n_used + max(1, n_used // SPARE_PAGE_DIVISOR)
    page_table = jax.random.permutation(next(keys), n_phys)[:n_used].reshape(DEC_BATCH, n_pages).astype(jnp.int32)
    d_in = D_MODEL ** -0.5
    return {
        'x_prompt': normal((BATCH, SEQ, D_MODEL)),
        'x_sample': normal((DEC_BATCH, DEC_SEQ, D_MODEL)),
        'state_pool_l0': normal((DEC_BATCH, POOL_STATE, D_INNER)),
        'cache_k_l1': normal((n_phys, PAGE_SIZE, ATT_HEADS, 2, ATT_HEAD_DIM)),
        'cache_v_l1': normal((n_phys, PAGE_SIZE, ATT_HEADS, ATT_VDIM)),
        'state_conv_l2': normal((DEC_BATCH, CONV_STATE, D_INNER), 0.5),
        'state_pool_l3': normal((DEC_BATCH, POOL_STATE, D_INNER)),
        'page_table': page_table,
        'norm_g_l0': gain(D_MODEL),
        'w_in_l0': normal((D_MODEL, 2 * D_INNER), d_in),
        'w_grp_l0': normal((N_POOL_GROUPS, POOL_GROUP, POOL_GROUP), POOL_GROUP ** -0.5),
        'pool_scale_l0': gain(D_INNER),
        'w_out_l0': normal((D_INNER, D_MODEL), D_INNER ** -0.5),
        'norm_g_l1': gain(D_MODEL),
        'w_in_l1': normal((D_MODEL, 2 * ATT_QK_WIDTH + 2 * ATT_WIDTH), d_in),
        'q_norm_g_l1': gain(ATT_HEAD_DIM),
        'k_norm_g_l1': gain(ATT_HEAD_DIM),
        'lambda_q1_l1': normal((ATT_HEAD_DIM,), 0.1),
        'lambda_k1_l1': normal((ATT_HEAD_DIM,), 0.1),
        'lambda_q2_l1': normal((ATT_HEAD_DIM,), 0.1),
        'lambda_k2_l1': normal((ATT_HEAD_DIM,), 0.1),
        'subln_g_l1': gain(ATT_VDIM),
        'w_out_l1': normal((ATT_WIDTH, D_MODEL), ATT_WIDTH ** -0.5),
        'norm_g_l2': gain(D_MODEL),
        'w_in_l2': normal((D_MODEL, 3 * D_INNER), d_in),
        'conv_w_l2': normal((CONV_WIDTH, D_INNER), CONV_WIDTH ** -0.5),
        'conv_b_l2': normal((D_INNER,), 0.01),
        'ln_g_l2': gain(D_INNER),
        'ln_b_l2': normal((D_INNER,), 0.01),
        'w_out_l2': normal((D_INNER, D_MODEL), D_INNER ** -0.5),
        'norm_g_l3': gain(D_MODEL),
        'w_in_l3': normal((D_MODEL, 2 * D_INNER), d_in),
        'w_grp_l3': normal((N_POOL_GROUPS, POOL_GROUP, POOL_GROUP), POOL_GROUP ** -0.5),
        'pool_scale_l3': gain(D_INNER),
        'w_out_l3': normal((D_INNER, D_MODEL), D_INNER ** -0.5),
    }


def reference(x_prompt, x_sample, state_pool_l0, cache_k_l1, cache_v_l1, state_conv_l2, state_pool_l3, page_table,
              norm_g_l0, w_in_l0, w_grp_l0, pool_scale_l0, w_out_l0,
              norm_g_l1, w_in_l1, q_norm_g_l1, k_norm_g_l1, lambda_q1_l1, lambda_k1_l1, lambda_q2_l1, lambda_k2_l1,
              subln_g_l1, w_out_l1,
              norm_g_l2, w_in_l2, conv_w_l2, conv_b_l2, ln_g_l2, ln_b_l2, w_out_l2,
              norm_g_l3, w_in_l3, w_grp_l3, pool_scale_l3, w_out_l3):
    lam = diff_lambda(lambda_q1_l1, lambda_k1_l1, lambda_q2_l1, lambda_k2_l1)
    layer_params = [
        (norm_g_l0, w_in_l0, w_grp_l0, pool_scale_l0, w_out_l0),
        (norm_g_l1, w_in_l1, q_norm_g_l1, k_norm_g_l1, subln_g_l1, w_out_l1),
        (norm_g_l2, w_in_l2, conv_w_l2, conv_b_l2, ln_g_l2, ln_b_l2, w_out_l2),
        (norm_g_l3, w_in_l3, w_grp_l3, pool_scale_l3, w_out_l3),
    ]
    layer_state = [state_pool_l0, (cache_k_l1, cache_v_l1), state_conv_l2, state_pool_l3]

    xp, xs = x_prompt, x_sample
    new_state = []
    for i in range(DEPTH):
        kind = MIXER_ORDER[i % N_MIXERS]
        params, carry = layer_params[i], layer_state[i]
        if kind == 'pool':
            xp, sp = pool_layer(xp, None, *params)
            xs, ss = pool_layer(xs, carry, *params)
            new_state += [sp, ss]
        elif kind == 'attn':
            xp, kp, vp = diff_attn_prompt(xp, lam, *params)
            xs, ks, vs = diff_attn_sample(xs, carry[0], carry[1], page_table, lam, *params)
            new_state += [kp, vp, ks, vs]
        else:
            xp, sp = conv_layer(xp, None, *params)
            xs, ss = conv_layer(xs, carry, *params)
            new_state += [sp, ss]

    (new_pool_l0_prompt, new_pool_l0_sample,
     new_k_l1_prompt, new_v_l1_prompt, new_k_l1_sample, new_v_l1_sample,
     new_conv_l2_prompt, new_conv_l2_sample,
     new_pool_l3_prompt, new_pool_l3_sample) = new_state
    return (xp, xs, new_pool_l0_prompt, new_pool_l0_sample,
            new_k_l1_prompt, new_v_l1_prompt, new_k_l1_sample, new_v_l1_sample,
            new_conv_l2_prompt, new_conv_l2_sample, new_pool_l3_prompt, new_pool_l3_sample)
```

```python
import functools
import math

import jax
import jax.numpy as jnp
from jax import lax
from jax.experimental import pallas as pl
from jax.experimental.pallas import tpu as pltpu

F32 = jnp.float32
BF16 = jnp.bfloat16

D_MODEL = 2048
D_INNER = 4096
POOL_WINDOWS = (2, 4, 8, 16)
POOL_GROUP = 1024
POOL_STATE = 15
ATT_HEADS = 16
ATT_HEAD_DIM = 64
ATT_VDIM = 128
LAMBDA_INIT = 0.8 - 0.6 * math.exp(-0.3 * 1)
CONV_WIDTH = 31
CONV_STATE = 30
NORM_EPS = 1e-6
NEG_INF = -1e30
PAGE_SIZE = 128

VMEM_LIMIT_BYTES = 58 * 1024 * 1024
LANE = 128
COL_BLOCK = 1024
POOL_HEAD = 16
CONV_HEAD = 32
ROW_CHUNK = 32
LANE_CHUNK = 256


def _params(sem):
    return pltpu.CompilerParams(dimension_semantics=sem, vmem_limit_bytes=VMEM_LIMIT_BYTES)


def _rms_rows(x_ref, g_ref, h_ref):
    xf = x_ref[...]
    ms = jnp.mean(xf * xf, axis=-1, keepdims=True)
    h_ref[...] = (xf * lax.rsqrt(ms + NORM_EPS) * g_ref[...]).astype(BF16)


def _silu(x):
    return x * jax.nn.sigmoid(x)


def _pool_in_kernel(*refs, nb, tb, tiles_per_batch, has_state):
    if has_state:
        (x_ref, ng_ref, wu_ref, wg_ref, wgrp_ref, sc_ref, st_in_ref,
         z_ref, st_ref, h_s, ext_s, pooled_s) = refs
        carry_s = None
    else:
        (x_ref, ng_ref, wu_ref, wg_ref, wgrp_ref, sc_ref,
         z_ref, st_ref, h_s, ext_s, pooled_s, carry_s) = refs
    i = pl.program_id(0)
    g = pl.program_id(1)

    @pl.when(g == 0)
    def _():
        _rms_rows(x_ref, ng_ref, h_s)

    u = jnp.dot(h_s[...], wu_ref[...], preferred_element_type=F32)

    if has_state:
        ext_s[:, 0:1, :] = jnp.zeros((nb, 1, COL_BLOCK), F32)
        ext_s[:, 1:POOL_HEAD, :] = st_in_ref[...]
    else:
        first = (i % tiles_per_batch) == 0

        @pl.when(first)
        def _():
            ext_s[0, 0:POOL_HEAD, :] = jnp.zeros((POOL_HEAD, COL_BLOCK), F32)

        @pl.when(jnp.logical_not(first))
        def _():
            ext_s[0, 0:POOL_HEAD, :] = carry_s[g]

    for b in range(nb):
        ext_s[b, POOL_HEAD:POOL_HEAD + tb, :] = u[b * tb:(b + 1) * tb, :]

    pos0 = POOL_STATE if has_state else (i % tiles_per_batch) * tb
    rc = min(tb, ROW_CHUNK)
    n_lane = COL_BLOCK // LANE_CHUNK
    n_row = tb // rc

    for k, w in enumerate(POOL_WINDOWS):
        @pl.when(g == k)
        def _(w=w):
            def chunk(it, carry):
                b = it // (n_row * n_lane)
                r0 = pl.multiple_of(((it // n_lane) % n_row) * rc, rc)
                l0 = pl.multiple_of((it % n_lane) * LANE_CHUNK, LANE_CHUNK)
                blk = ext_s[b, pl.ds(r0, rc + POOL_HEAD), pl.ds(l0, LANE_CHUNK)]
                cur = blk[POOL_HEAD:POOL_HEAD + rc, :]
                acc = cur
                for j in range(1, w):
                    acc = acc + blk[POOL_HEAD - j:POOL_HEAD - j + rc, :]
                n_seen = pos0 + r0 + 1 + lax.broadcasted_iota(jnp.int32, (rc, LANE_CHUNK), 0)
                cnt = jnp.minimum(n_seen, w).astype(F32)
                mean = jnp.where(n_seen >= w, acc * (1.0 / w), acc / cnt)
                pooled_s[pl.ds(b * tb + r0, rc), pl.ds(l0, LANE_CHUNK)] = mean - cur
                return carry

            lax.fori_loop(0, nb * n_row * n_lane, chunk, 0)

    col0 = pl.multiple_of(g * COL_BLOCK, COL_BLOCK)
    if has_state:
        for b in range(nb):
            st_ref[b, :, pl.ds(col0, COL_BLOCK)] = ext_s[b, tb + 1:tb + POOL_HEAD, :]
    else:
        @pl.when((i % tiles_per_batch) == tiles_per_batch - 1)
        def _():
            st_ref[i // tiles_per_batch, :, pl.ds(col0, COL_BLOCK)] = ext_s[0, tb + 1:tb + POOL_HEAD, :]

        carry_s[g] = ext_s[0, tb:tb + POOL_HEAD, :]

    mixed = jnp.dot(pooled_s[...].astype(BF16), wgrp_ref[0], preferred_element_type=F32) * sc_ref[...]
    gate = jnp.dot(h_s[...], wg_ref[...], preferred_element_type=F32)
    z_ref[...] = (mixed * _silu(gate)).astype(BF16)


def _pool_in(x2d, norm_g, w_in, w_grp, scale, state, *, n_batch, seq, tm):
    m = x2d.shape[0]
    has_state = state is not None
    if has_state:
        assert tm == m
        nb, tb, tiles_per_batch = n_batch, seq, 1
    else:
        assert seq % tm == 0 and tm >= POOL_HEAD
        nb, tb, tiles_per_batch = 1, tm, seq // tm
    n_groups = len(POOL_WINDOWS)
    in_specs = [
        pl.BlockSpec((tm, D_MODEL), lambda i, g: (i, 0)),
        pl.BlockSpec((1, D_MODEL), lambda i, g: (0, 0)),
        pl.BlockSpec((D_MODEL, COL_BLOCK), lambda i, g: (0, g)),
        pl.BlockSpec((D_MODEL, COL_BLOCK), lambda i, g: (0, n_groups + g)),
        pl.BlockSpec((1, POOL_GROUP, POOL_GROUP), lambda i, g: (g, 0, 0)),
        pl.BlockSpec((1, COL_BLOCK), lambda i, g: (0, g)),
    ]
    args = [x2d, norm_g.reshape(1, D_MODEL), w_in, w_in, w_grp, scale.reshape(1, D_INNER)]
    scratch = [
        pltpu.VMEM((tm, D_MODEL), BF16),
        pltpu.VMEM((nb, POOL_HEAD + tb, COL_BLOCK), F32),
        pltpu.VMEM((tm, COL_BLOCK), F32),
    ]
    if has_state:
        in_specs.append(pl.BlockSpec((nb, POOL_STATE, COL_BLOCK), lambda i, g: (0, 0, g)))
        args.append(state)
    else:
        scratch.append(pltpu.VMEM((n_groups, POOL_HEAD, COL_BLOCK), F32))
    kern = functools.partial(_pool_in_kernel, nb=nb, tb=tb, tiles_per_batch=tiles_per_batch,
                             has_state=has_state)
    return pl.pallas_call(
        kern,
        grid=(m // tm, n_groups),
        in_specs=in_specs,
        out_specs=[
            pl.BlockSpec((tm, COL_BLOCK), lambda i, g: (i, g)),
            pl.BlockSpec((n_batch, POOL_STATE, D_INNER), lambda i, g: (0, 0, 0)),
        ],
        out_shape=[
            jax.ShapeDtypeStruct((m, D_INNER), BF16),
            jax.ShapeDtypeStruct((n_batch, POOL_STATE, D_INNER), F32),
        ],
        scratch_shapes=scratch,
        compiler_params=_params(("arbitrary", "arbitrary")),
        name="pool_in",
    )(*args)


def _out_proj_kernel(z_ref, w_ref, x_ref, o_ref):
    o_ref[...] = x_ref[...] + jnp.dot(z_ref[...], w_ref[...], preferred_element_type=F32)


def _out_proj(z, w, x2d, *, tm, tn):
    m, k = z.shape
    n = w.shape[1]
    return pl.pallas_call(
        _out_proj_kernel,
        grid=(m // tm, n // tn),
        in_specs=[
            pl.BlockSpec((tm, k), lambda i, j: (i, 0)),
            pl.BlockSpec((k, tn), lambda i, j: (0, j)),
            pl.BlockSpec((tm, tn), lambda i, j: (i, j)),
        ],
        out_specs=pl.BlockSpec((tm, tn), lambda i, j: (i, j)),
        out_shape=jax.ShapeDtypeStruct((m, n), F32),
        compiler_params=_params(("arbitrary", "arbitrary")),
        name="out_proj",
    )(z, w, x2d)


def _ln_out_proj_kernel(y_ref, gate_ref, lg_ref, lb_ref, w_ref, x_ref, o_ref, z_s):
    @pl.when(pl.program_id(1) == 0)
    def _():
        y = y_ref[...]
        yc = y - jnp.mean(y, axis=-1, keepdims=True)
        var = jnp.mean(yc * yc, axis=-1, keepdims=True)
        ln = yc * lax.rsqrt(var + NORM_EPS) * lg_ref[...] + lb_ref[...]
        z_s[...] = (_silu(ln) * _silu(gate_ref[...])).astype(BF16)

    o_ref[...] = x_ref[...] + jnp.dot(z_s[...], w_ref[...], preferred_element_type=F32)


def _ln_out_proj(y, gate, ln_g, ln_b, w, x2d, *, tm, tn):
    m, k = y.shape
    n = w.shape[1]
    return pl.pallas_call(
        _ln_out_proj_kernel,
        grid=(m // tm, n // tn),
        in_specs=[
            pl.BlockSpec((tm, k), lambda i, j: (i, 0)),
            pl.BlockSpec((tm, k), lambda i, j: (i, 0)),
            pl.BlockSpec((1, k), lambda i, j: (0, 0)),
            pl.BlockSpec((1, k), lambda i, j: (0, 0)),
            pl.BlockSpec((k, tn), lambda i, j: (0, j)),
            pl.BlockSpec((tm, tn), lambda i, j: (i, j)),
        ],
        out_specs=pl.BlockSpec((tm, tn), lambda i, j: (i, j)),
        out_shape=jax.ShapeDtypeStruct((m, n), F32),
        scratch_shapes=[pltpu.VMEM((tm, k), BF16)],
        compiler_params=_params(("arbitrary", "arbitrary")),
        name="ln_out_proj",
    )(y, gate, ln_g.reshape(1, k), ln_b.reshape(1, k), w, x2d)


def _head_norm(y, g128):
    lane = lax.broadcasted_iota(jnp.int32, (1, LANE), 1)
    low = lane < ATT_HEAD_DIM
    outs = []
    for c in range(y.shape[1] // LANE):
        blk = y[:, c * LANE:(c + 1) * LANE]
        sq = blk * blk
        s_lo = jnp.sum(jnp.where(low, sq, 0.0), axis=-1, keepdims=True)
        s_hi = jnp.sum(jnp.where(low, 0.0, sq), axis=-1, keepdims=True)
        ms = jnp.where(low, s_lo, s_hi) * (1.0 / ATT_HEAD_DIM)
        outs.append(blk * lax.rsqrt(ms + NORM_EPS) * g128)
    return jnp.concatenate(outs, axis=1)


def _attn_in_kernel(x_ref, ng_ref, w_ref, qg_ref, kg_ref, q_ref, k_ref, v_ref, gate_ref, h_s,
                    *, q_dtype):
    j = pl.program_id(1)

    @pl.when(j == 0)
    def _():
        _rms_rows(x_ref, ng_ref, h_s)

    y = jnp.dot(h_s[...], w_ref[...], preferred_element_type=F32)

    @pl.when(j < 2)
    def _():
        q_ref[...] = (_head_norm(y, qg_ref[...]) * (ATT_HEAD_DIM ** -0.5)).astype(q_dtype)

    @pl.when(jnp.logical_and(j >= 2, j < 4))
    def _():
        k_ref[...] = _head_norm(y, kg_ref[...])

    @pl.when(jnp.logical_and(j >= 4, j < 6))
    def _():
        v_ref[...] = y

    @pl.when(j >= 6)
    def _():
        gate_ref[...] = y


def _attn_in(x2d, norm_g, w_in, q_norm_g, k_norm_g, *, tm, q_dtype):
    m = x2d.shape[0]
    n_col = w_in.shape[1] // COL_BLOCK
    kern = functools.partial(_attn_in_kernel, q_dtype=q_dtype)

    def out_spec(first):
        return pl.BlockSpec((tm, COL_BLOCK), lambda i, j: (i, jnp.clip(j - first, 0, 1)))

    return pl.pallas_call(
        kern,
        grid=(m // tm, n_col),
        in_specs=[
            pl.BlockSpec((tm, D_MODEL), lambda i, j: (i, 0)),
            pl.BlockSpec((1, D_MODEL), lambda i, j: (0, 0)),
            pl.BlockSpec((D_MODEL, COL_BLOCK), lambda i, j: (0, j)),
            pl.BlockSpec((1, LANE), lambda i, j: (0, 0)),
            pl.BlockSpec((1, LANE), lambda i, j: (0, 0)),
        ],
        out_specs=[out_spec(0), out_spec(2), out_spec(4), out_spec(6)],
        out_shape=[
            jax.ShapeDtypeStruct((m, D_MODEL), q_dtype),
            jax.ShapeDtypeStruct((m, D_MODEL), F32),
            jax.ShapeDtypeStruct((m, D_MODEL), F32),
            jax.ShapeDtypeStruct((m, D_MODEL), F32),
        ],
        scratch_shapes=[pltpu.VMEM((tm, D_MODEL), BF16)],
        compiler_params=_params(("arbitrary", "arbitrary")),
        name="attn_in",
    )(x2d, norm_g.reshape(1, D_MODEL), w_in, jnp.tile(q_norm_g, 2).reshape(1, LANE),
      jnp.tile(k_norm_g, 2).reshape(1, LANE))


def _lambda(lq1_ref, lk1_ref, lq2_ref, lk2_ref):
    a = jnp.sum(lq1_ref[...] * lk1_ref[...], axis=-1, keepdims=True)
    b = jnp.sum(lq2_ref[...] * lk2_ref[...], axis=-1, keepdims=True)
    return jnp.exp(a) - jnp.exp(b) + LAMBDA_INIT


def _sub_ln_gate(o, subln_g, gate):
    ms = jnp.mean(o * o, axis=-1, keepdims=True)
    o = o * lax.rsqrt(ms + NORM_EPS) * subln_g * (1.0 - LAMBDA_INIT)
    return o * _silu(gate)


def _dot_nt(a, b):
    return lax.dot_general(a, b, (((1,), (1,)), ((), ())), preferred_element_type=F32)


def _attn_prompt_kernel(q_ref, k_ref, v_ref, gate_ref, sg_ref, lq1_ref, lk1_ref, lq2_ref, lk2_ref,
                        z_ref, kb_s, vb_s, m_s, l_s, acc_s, *, tq):
    h = pl.program_id(1)
    qi = pl.program_id(2)

    @pl.when(qi == 0)
    def _():
        kb_s[...] = k_ref[...].astype(BF16)
        vb_s[...] = v_ref[...].astype(BF16)

    q = q_ref[...]
    lane = lax.broadcasted_iota(jnp.int32, (1, LANE), 1)
    low = lane < ATT_HEAD_DIM
    zero = jnp.zeros_like(q)
    qc = (jnp.where(low, q, zero), jnp.where(low, zero, q))

    slope = jnp.exp2(-0.5 * jnp.full((1, tq), h + 1, jnp.int32).astype(F32))
    col = lax.broadcasted_iota(jnp.int32, (1, tq), 1)

    m_s[...] = jnp.full(m_s.shape, NEG_INF, F32)
    l_s[...] = jnp.zeros(l_s.shape, F32)
    acc_s[...] = jnp.zeros(acc_s.shape, F32)

    def tile(kj, masked):
        start = pl.multiple_of(kj * tq, tq)
        kt = kb_s[pl.ds(start, tq), :]
        vt = vb_s[pl.ds(start, tq), :]
        bias = slope * (col + (kj - qi) * tq).astype(F32)
        for c in range(2):
            s = _dot_nt(qc[c], kt) + bias
            if masked:
                row_i = lax.broadcasted_iota(jnp.int32, (tq, tq), 0)
                col_j = lax.broadcasted_iota(jnp.int32, (tq, tq), 1)
                s = jnp.where(col_j <= row_i, s, NEG_INF)
            m_old = m_s[c]
            m_new = jnp.maximum(m_old, jnp.max(s, axis=-1, keepdims=True))
            alpha = jnp.exp(m_old - m_new)
            p = jnp.exp(s - m_new)
            l_s[c] = alpha * l_s[c] + jnp.sum(p, axis=-1, keepdims=True)
            acc_s[c] = alpha * acc_s[c] + jnp.dot(p.astype(BF16), vt, preferred_element_type=F32)
            m_s[c] = m_new

    def body(kj, carry):
        tile(kj, False)
        return carry

    lax.fori_loop(0, qi, body, 0)
    tile(qi, True)

    lam = _lambda(lq1_ref, lk1_ref, lq2_ref, lk2_ref)
    o = acc_s[0] / l_s[0] - lam * (acc_s[1] / l_s[1])
    z_ref[...] = _sub_ln_gate(o, sg_ref[...], gate_ref[...]).astype(BF16)


def _attn_prompt(q, k, v, gate, subln_g, lams, *, n_batch, seq, tq):
    vec = pl.BlockSpec((1, ATT_HEAD_DIM), lambda b, h, i: (0, 0))
    kern = functools.partial(_attn_prompt_kernel, tq=tq)
    return pl.pallas_call(
        kern,
        grid=(n_batch, ATT_HEADS, seq // tq),
        in_specs=[
            pl.BlockSpec((None, tq, LANE), lambda b, h, i: (b, i, h)),
            pl.BlockSpec((None, seq, LANE), lambda b, h, i: (b, 0, h)),
            pl.BlockSpec((None, seq, LANE), lambda b, h, i: (b, 0, h)),
            pl.BlockSpec((None, tq, LANE), lambda b, h, i: (b, i, h)),
            pl.BlockSpec((1, LANE), lambda b, h, i: (0, 0)),
            vec, vec, vec, vec,
        ],
        out_specs=pl.BlockSpec((None, tq, LANE), lambda b, h, i: (b, i, h)),
        out_shape=jax.ShapeDtypeStruct((n_batch, seq, D_MODEL), BF16),
        scratch_shapes=[
            pltpu.VMEM((seq, LANE), BF16),
            pltpu.VMEM((seq, LANE), BF16),
            pltpu.VMEM((2, tq, 1), F32),
            pltpu.VMEM((2, tq, 1), F32),
            pltpu.VMEM((2, tq, LANE), F32),
        ],
        compiler_params=_params(("arbitrary", "arbitrary", "arbitrary")),
        name="attn_prompt",
    )(q, k, v, gate, subln_g.reshape(1, LANE), *[t.reshape(1, ATT_HEAD_DIM) for t in lams])


def _attn_sample_kernel(pt_ref, q_ref, kn_ref, vn_ref, gate_ref, sg_ref,
                        lq1_ref, lk1_ref, lq2_ref, lk2_ref, *rest, pages, n_steps, past_len, t_new):
    k_refs = rest[:pages]
    v_refs = rest[pages:2 * pages]
    z_ref, qbd_s, m_s, l_s, acc_s = rest[2 * pages:]
    p = pl.program_id(1)
    n_rows = 2 * ATT_HEADS * t_new

    row = lax.broadcasted_iota(jnp.int32, (n_rows, 1), 0)
    row_head = (row % (ATT_HEADS * t_new)) // t_new
    row_comp = row // (ATT_HEADS * t_new)
    row_t = row % t_new
    slope = jnp.exp2(-0.5 * (row_head + 1).astype(F32))

    @pl.when(p == 0)
    def _():
        col = lax.broadcasted_iota(jnp.int32, (1, D_MODEL), 1)
        keep = jnp.logical_and(col // LANE == row_head, (col % LANE) // ATT_HEAD_DIM == row_comp)
        q_rep = jnp.concatenate([q_ref[...]] * (n_rows // t_new), axis=0)
        qbd_s[...] = jnp.where(keep, q_rep, 0.0).astype(BF16)
        m_s[...] = jnp.full(m_s.shape, NEG_INF, F32)
        l_s[...] = jnp.zeros(l_s.shape, F32)
        acc_s[...] = jnp.zeros(acc_s.shape, F32)

    qbd = qbd_s[...]

    def update(s, vb):
        m_old = m_s[...]
        m_new = jnp.maximum(m_old, jnp.max(s, axis=-1, keepdims=True))
        alpha = jnp.exp(m_old - m_new)
        pr = jnp.exp(s - m_new)
        l_s[...] = alpha * l_s[...] + jnp.sum(pr, axis=-1, keepdims=True)
        acc_s[...] = alpha * acc_s[...] + jnp.dot(pr.astype(BF16), vb, preferred_element_type=F32)
        m_s[...] = m_new

    tok = lax.broadcasted_iota(jnp.int32, (1, PAGE_SIZE), 1)
    for r in range(pages):
        k_pos = (p * pages + r) * PAGE_SIZE + tok - past_len
        s = _dot_nt(qbd, k_refs[r][...].astype(BF16)) + slope * k_pos.astype(F32)
        update(s, v_refs[r][...].astype(BF16))

    @pl.when(p == n_steps - 1)
    def _():
        tok_new = lax.broadcasted_iota(jnp.int32, (1, t_new), 1)
        s = _dot_nt(qbd, kn_ref[...].astype(BF16)) + slope * tok_new.astype(F32)
        s = jnp.where(tok_new <= row_t, s, NEG_INF)
        update(s, vn_ref[...].astype(BF16))

        lam = _lambda(lq1_ref, lk1_ref, lq2_ref, lk2_ref)
        half = ATT_HEADS * t_new
        o_all = acc_s[0:half, :] / l_s[0:half, :] - lam * (acc_s[half:, :] / l_s[half:, :])
        for hh in range(ATT_HEADS):
            o = o_all[hh * t_new:(hh + 1) * t_new, hh * LANE:(hh + 1) * LANE]
            gate = gate_ref[:, hh * LANE:(hh + 1) * LANE]
            z_ref[:, hh * LANE:(hh + 1) * LANE] = _sub_ln_gate(o, sg_ref[...], gate)


def _attn_sample(q, k_new, v_new, gate, cache_k, cache_v, page_table, subln_g, lams, *, pages):
    n_batch, t_new, _ = q.shape
    n_pages = page_table.shape[1]
    n_steps = n_pages // pages
    n_phys = cache_k.shape[0]
    ck = cache_k.reshape(n_phys, PAGE_SIZE, D_MODEL)
    cv = cache_v.reshape(n_phys, PAGE_SIZE, D_MODEL)
    n_rows = 2 * ATT_HEADS * t_new

    tok_spec = pl.BlockSpec((None, t_new, D_MODEL), lambda b, p, pt: (b, 0, 0))
    vec = pl.BlockSpec((1, ATT_HEAD_DIM), lambda b, p, pt: (0, 0))

    def page_spec(r):
        return pl.BlockSpec((None, PAGE_SIZE, D_MODEL), lambda b, p, pt: (pt[b, p * pages + r], 0, 0))

    kern = functools.partial(_attn_sample_kernel, pages=pages, n_steps=n_steps,
                             past_len=n_pages * PAGE_SIZE, t_new=t_new)
    grid_spec = pltpu.PrefetchScalarGridSpec(
        num_scalar_prefetch=1,
        grid=(n_batch, n_steps),
        in_specs=[tok_spec, tok_spec, tok_spec, tok_spec,
                  pl.BlockSpec((1, LANE), lambda b, p, pt: (0, 0)), vec, vec, vec, vec]
        + [page_spec(r) for r in range(pages)] + [page_spec(r) for r in range(pages)],
        out_specs=tok_spec,
        scratch_shapes=[
            pltpu.VMEM((n_rows, D_MODEL), BF16),
            pltpu.VMEM((n_rows, 1), F32),
            pltpu.VMEM((n_rows, 1), F32),
            pltpu.VMEM((n_rows, D_MODEL), F32),
        ],
    )
    return pl.pallas_call(
        kern,
        grid_spec=grid_spec,
        out_shape=jax.ShapeDtypeStruct((n_batch, t_new, D_MODEL), F32),
        compiler_params=_params(("arbitrary", "arbitrary")),
        name="attn_sample",
    )(page_table, q, k_new, v_new, gate, subln_g.reshape(1, LANE),
      *[t.reshape(1, ATT_HEAD_DIM) for t in lams], *([ck] * pages), *([cv] * pages))


def _conv_in_kernel(*refs, nb, tb, tiles_per_batch, has_state):
    if has_state:
        (x_ref, ng_ref, wa_ref, wl_ref, wg_ref, cw_ref, cb_ref, st_in_ref,
         y_ref, gate_ref, st_ref, h_s, ext_s) = refs
        carry_s = None
    else:
        (x_ref, ng_ref, wa_ref, wl_ref, wg_ref, cw_ref, cb_ref,
         y_ref, gate_ref, st_ref, h_s, ext_s, carry_s) = refs
    i = pl.program_id(0)
    cb = pl.program_id(1)

    @pl.when(cb == 0)
    def _():
        _rms_rows(x_ref, ng_ref, h_s)

    a = jnp.dot(h_s[...], wa_ref[...], preferred_element_type=F32)
    glu = jnp.dot(h_s[...], wl_ref[...], preferred_element_type=F32)
    c = a * jax.nn.sigmoid(glu)

    head0 = CONV_HEAD - CONV_STATE
    if has_state:
        ext_s[:, 0:head0, :] = jnp.zeros((nb, head0, COL_BLOCK), F32)
        ext_s[:, head0:CONV_HEAD, :] = st_in_ref[...]
    else:
        first = (i % tiles_per_batch) == 0

        @pl.when(first)
        def _():
            ext_s[0, 0:CONV_HEAD, :] = jnp.zeros((CONV_HEAD, COL_BLOCK), F32)

        @pl.when(jnp.logical_not(first))
        def _():
            ext_s[0, 0:CONV_HEAD, :] = carry_s[cb]

    for b in range(nb):
        ext_s[b, CONV_HEAD:CONV_HEAD + tb, :] = c[b * tb:(b + 1) * tb, :]

    rc = min(tb, ROW_CHUNK)
    n_lane = COL_BLOCK // LANE_CHUNK
    n_row = tb // rc

    def chunk(it, carry):
        b = it // (n_row * n_lane)
        r0 = pl.multiple_of(((it // n_lane) % n_row) * rc, rc)
        l0 = pl.multiple_of((it % n_lane) * LANE_CHUNK, LANE_CHUNK)
        blk = ext_s[b, pl.ds(r0, rc + CONV_HEAD), pl.ds(l0, LANE_CHUNK)]
        acc = jnp.broadcast_to(cb_ref[:, pl.ds(l0, LANE_CHUNK)], (rc, LANE_CHUNK))
        for j in range(CONV_WIDTH):
            acc = acc + cw_ref[j:j + 1, pl.ds(l0, LANE_CHUNK)] * blk[head0 + j:head0 + j + rc, :]
        y_ref[pl.ds(b * tb + r0, rc), pl.ds(l0, LANE_CHUNK)] = acc
        return carry

    lax.fori_loop(0, nb * n_row * n_lane, chunk, 0)
    col0 = pl.multiple_of(cb * COL_BLOCK, COL_BLOCK)
    if has_state:
        for b in range(nb):
            st_ref[b, :, pl.ds(col0, COL_BLOCK)] = ext_s[b, tb + head0:tb + CONV_HEAD, :]
    else:
        @pl.when((i % tiles_per_batch) == tiles_per_batch - 1)
        def _():
            st_ref[i // tiles_per_batch, :, pl.ds(col0, COL_BLOCK)] = ext_s[0, tb + head0:tb + CONV_HEAD, :]

        carry_s[cb] = ext_s[0, tb:tb + CONV_HEAD, :]

    gate_ref[...] = jnp.dot(h_s[...], wg_ref[...], preferred_element_type=F32)


def _conv_in(x2d, norm_g, w_in, conv_w, conv_b, state, *, n_batch, seq, tm):
    m = x2d.shape[0]
    has_state = state is not None
    if has_state:
        assert tm == m
        nb, tb, tiles_per_batch = n_batch, seq, 1
    else:
        assert seq % tm == 0 and tm >= CONV_HEAD
        nb, tb, tiles_per_batch = 1, tm, seq // tm
    n_cb = D_INNER // COL_BLOCK
    in_specs = [
        pl.BlockSpec((tm, D_MODEL), lambda i, c: (i, 0)),
        pl.BlockSpec((1, D_MODEL), lambda i, c: (0, 0)),
        pl.BlockSpec((D_MODEL, COL_BLOCK), lambda i, c: (0, c)),
        pl.BlockSpec((D_MODEL, COL_BLOCK), lambda i, c: (0, n_cb + c)),
        pl.BlockSpec((D_MODEL, COL_BLOCK), lambda i, c: (0, 2 * n_cb + c)),
        pl.BlockSpec((CONV_WIDTH, COL_BLOCK), lambda i, c: (0, c)),
        pl.BlockSpec((1, COL_BLOCK), lambda i, c: (0, c)),
    ]
    args = [x2d, norm_g.reshape(1, D_MODEL), w_in, w_in, w_in, conv_w, conv_b.reshape(1, D_INNER)]
    scratch = [
        pltpu.VMEM((tm, D_MODEL), BF16),
        pltpu.VMEM((nb, CONV_HEAD + tb, COL_BLOCK), F32),
    ]
    if has_state:
        in_specs.append(pl.BlockSpec((nb, CONV_STATE, COL_BLOCK), lambda i, c: (0, 0, c)))
        args.append(state)
    else:
        scratch.append(pltpu.VMEM((n_cb, CONV_HEAD, COL_BLOCK), F32))
    kern = functools.partial(_conv_in_kernel, nb=nb, tb=tb, tiles_per_batch=tiles_per_batch,
                             has_state=has_state)
    return pl.pallas_call(
        kern,
        grid=(m // tm, n_cb),
        in_specs=in_specs,
        out_specs=[
            pl.BlockSpec((tm, COL_BLOCK), lambda i, c: (i, c)),
            pl.BlockSpec((tm, COL_BLOCK), lambda i, c: (i, c)),
            pl.BlockSpec((n_batch, CONV_STATE, D_INNER), lambda i, c: (0, 0, 0)),
        ],
        out_shape=[
            jax.ShapeDtypeStruct((m, D_INNER), F32),
            jax.ShapeDtypeStruct((m, D_INNER), F32),
            jax.ShapeDtypeStruct((n_batch, CONV_STATE, D_INNER), F32),
        ],
        scratch_shapes=scratch,
        compiler_params=_params(("arbitrary", "arbitrary")),
        name="conv_in",
    )(*args)


PROMPT_TM = 512
ATTN_TQ = 512
SAMPLE_PAGES_PER_STEP = 4


def kernel(x_prompt, x_sample, state_pool_l0, cache_k_l1, cache_v_l1, state_conv_l2, state_pool_l3, page_table, norm_g_l0, w_in_l0, w_grp_l0, pool_scale_l0, w_out_l0, norm_g_l1, w_in_l1, q_norm_g_l1, k_norm_g_l1, lambda_q1_l1, lambda_k1_l1, lambda_q2_l1, lambda_k2_l1, subln_g_l1, w_out_l1, norm_g_l2, w_in_l2, conv_w_l2, conv_b_l2, ln_g_l2, ln_b_l2, w_out_l2, norm_g_l3, w_in_l3, w_grp_l3, pool_scale_l3, w_out_l3):
    nbp, seq, _ = x_prompt.shape
    nbs, t_new, _ = x_sample.shape
    mp, ms = nbp * seq, nbs * t_new
    xp = x_prompt.reshape(mp, D_MODEL)
    xs = x_sample.reshape(ms, D_MODEL)
    lams = (lambda_q1_l1, lambda_k1_l1, lambda_q2_l1, lambda_k2_l1)

    def bf(w):
        return w.astype(BF16)

    def pool_layer(xp, xs, state, norm_g, w_in, w_grp, scale, w_out):
        w_in, w_grp, w_out = bf(w_in), bf(w_grp), bf(w_out)
        zp, stp = _pool_in(xp, norm_g, w_in, w_grp, scale, None, n_batch=nbp, seq=seq, tm=PROMPT_TM)
        zs, sts = _pool_in(xs, norm_g, w_in, w_grp, scale, state, n_batch=nbs, seq=t_new, tm=ms)
        xp = _out_proj(zp, w_out, xp, tm=PROMPT_TM, tn=COL_BLOCK)
        xs = _out_proj(zs, w_out, xs, tm=ms, tn=COL_BLOCK)
        return xp, xs, stp, sts

    xp, xs, pool0_p, pool0_s = pool_layer(xp, xs, state_pool_l0, norm_g_l0, w_in_l0, w_grp_l0,
                                          pool_scale_l0, w_out_l0)

    w_in, w_out = bf(w_in_l1), bf(w_out_l1)
    qp, kp, vp, gp = _attn_in(xp, norm_g_l1, w_in, q_norm_g_l1, k_norm_g_l1, tm=PROMPT_TM, q_dtype=BF16)
    qs, ks, vs, gs = _attn_in(xs, norm_g_l1, w_in, q_norm_g_l1, k_norm_g_l1, tm=ms, q_dtype=F32)
    shp = (nbp, seq, D_MODEL)
    zp = _attn_prompt(qp.reshape(shp), kp.reshape(shp), vp.reshape(shp), gp.reshape(shp), subln_g_l1,
                      lams, n_batch=nbp, seq=seq, tq=ATTN_TQ)
    shs = (nbs, t_new, D_MODEL)
    zs = _attn_sample(qs.reshape(shs), ks.reshape(shs), vs.reshape(shs), gs.reshape(shs),
                      cache_k_l1, cache_v_l1, page_table, subln_g_l1, lams,
                      pages=SAMPLE_PAGES_PER_STEP)
    xp = _out_proj(zp.reshape(mp, D_MODEL), w_out, xp, tm=PROMPT_TM, tn=COL_BLOCK)
    xs = _out_proj(zs.reshape(ms, D_MODEL).astype(BF16), w_out, xs, tm=ms, tn=COL_BLOCK)
    new_k_p = kp.reshape(nbp, seq, ATT_HEADS, 2, ATT_HEAD_DIM)
    new_v_p = vp.reshape(nbp, seq, ATT_HEADS, ATT_VDIM)
    new_k_s = ks.reshape(nbs, t_new, ATT_HEADS, 2, ATT_HEAD_DIM)
    new_v_s = vs.reshape(nbs, t_new, ATT_HEADS, ATT_VDIM)

    w_in, w_out = bf(w_in_l2), bf(w_out_l2)
    yp, gp, conv_p = _conv_in(xp, norm_g_l2, w_in, conv_w_l2, conv_b_l2, None, n_batch=nbp, seq=seq,
                              tm=PROMPT_TM)
    ys, gs, conv_s = _conv_in(xs, norm_g_l2, w_in, conv_w_l2, conv_b_l2, state_conv_l2, n_batch=nbs,
                              seq=t_new, tm=ms)
    xp = _ln_out_proj(yp, gp, ln_g_l2, ln_b_l2, w_out, xp, tm=256, tn=512)
    xs = _ln_out_proj(ys, gs, ln_g_l2, ln_b_l2, w_out, xs, tm=ms, tn=512)

    xp, xs, pool3_p, pool3_s = pool_layer(xp, xs, state_pool_l3, norm_g_l3, w_in_l3, w_grp_l3,
                                          pool_scale_l3, w_out_l3)

    return (xp.reshape(nbp, seq, D_MODEL), xs.reshape(nbs, t_new, D_MODEL), pool0_p, pool0_s,
            new_k_p, new_v_p, new_k_s, new_v_s, conv_p, conv_s, pool3_p, pool3_s)
```

```python
import functools
import math

import jax
import jax.numpy as jnp
from jax import lax
from jax.experimental import pallas as pl
from jax.experimental.pallas import tpu as pltpu

F32 = jnp.float32
BF16 = jnp.bfloat16

D_MODEL = 2048
D_INNER = 4096
POOL_WINDOWS = (2, 4, 8, 16)
POOL_GROUP = 1024
POOL_STATE = 15
ATT_HEADS = 16
ATT_HEAD_DIM = 64
ATT_VDIM = 128
LAMBDA_INIT = 0.8 - 0.6 * math.exp(-0.3 * 1)
CONV_WIDTH = 31
CONV_STATE = 30
NORM_EPS = 1e-6
NEG_INF = -1e30
PAGE_SIZE = 128

VMEM_LIMIT_BYTES = 58 * 1024 * 1024
LANE = 128
SUBLANES = 8
BF16_ROWS = 16
COL_BLOCK = 1024
POOL_HEAD = 16
CONV_HEAD = 32
ROW_CHUNK = 32
LANE_CHUNK = 256


def _params(sem):
    return pltpu.CompilerParams(dimension_semantics=sem, vmem_limit_bytes=VMEM_LIMIT_BYTES)


def _rms_rows(x_ref, g_ref, h_ref):
    xf = x_ref[...]
    ms = jnp.mean(xf * xf, axis=-1, keepdims=True)
    h_ref[...] = (xf * lax.rsqrt(ms + NORM_EPS) * g_ref[...]).astype(BF16)


def _silu(x):
    return x * jax.nn.sigmoid(x)


def _pool_in_kernel(*refs, nb, tb, tiles_per_batch, has_state):
    if has_state:
        (x_ref, ng_ref, wu_ref, wg_ref, wgrp_ref, sc_ref, st_in_ref,
         z_ref, st_ref, h_s, ext_s, pooled_s) = refs
        carry_s = None
    else:
        (x_ref, ng_ref, wu_ref, wg_ref, wgrp_ref, sc_ref,
         z_ref, st_ref, h_s, ext_s, pooled_s, carry_s) = refs
    i = pl.program_id(0)
    g = pl.program_id(1)

    @pl.when(g == 0)
    def _():
        _rms_rows(x_ref, ng_ref, h_s)

    u = jnp.dot(h_s[...], wu_ref[...], preferred_element_type=F32)

    if has_state:
        ext_s[:, 0:1, :] = jnp.zeros((nb, 1, COL_BLOCK), F32)
        ext_s[:, 1:POOL_HEAD, :] = st_in_ref[...]
    else:
        first = (i % tiles_per_batch) == 0

        @pl.when(first)
        def _():
            ext_s[0, 0:POOL_HEAD, :] = jnp.zeros((POOL_HEAD, COL_BLOCK), F32)

        @pl.when(jnp.logical_not(first))
        def _():
            ext_s[0, 0:POOL_HEAD, :] = carry_s[g]

    for b in range(nb):
        ext_s[b, POOL_HEAD:POOL_HEAD + tb, :] = u[b * tb:(b + 1) * tb, :]

    pos0 = POOL_STATE if has_state else (i % tiles_per_batch) * tb
    rc = min(tb, ROW_CHUNK)
    n_lane = COL_BLOCK // LANE_CHUNK
    n_row = tb // rc

    for k, w in enumerate(POOL_WINDOWS):
        @pl.when(g == k)
        def _(w=w):
            def chunk(it, carry):
                b = it // (n_row * n_lane)
                r0 = pl.multiple_of(((it // n_lane) % n_row) * rc, rc)
                l0 = pl.multiple_of((it % n_lane) * LANE_CHUNK, LANE_CHUNK)
                blk = ext_s[b, pl.ds(r0, rc + POOL_HEAD), pl.ds(l0, LANE_CHUNK)]
                cur = blk[POOL_HEAD:POOL_HEAD + rc, :]
                acc = cur
                for j in range(1, w):
                    acc = acc + blk[POOL_HEAD - j:POOL_HEAD - j + rc, :]
                n_seen = pos0 + r0 + 1 + lax.broadcasted_iota(jnp.int32, (rc, LANE_CHUNK), 0)
                cnt = jnp.minimum(n_seen, w).astype(F32)
                mean = jnp.where(n_seen >= w, acc * (1.0 / w), acc / cnt)
                pooled_s[pl.ds(b * tb + r0, rc), pl.ds(l0, LANE_CHUNK)] = mean - cur
                return carry

            lax.fori_loop(0, nb * n_row * n_lane, chunk, 0)

    col0 = pl.multiple_of(g * COL_BLOCK, COL_BLOCK)
    if has_state:
        for b in range(nb):
            st_ref[b, :, pl.ds(col0, COL_BLOCK)] = ext_s[b, tb + 1:tb + POOL_HEAD, :]
    else:
        @pl.when((i % tiles_per_batch) == tiles_per_batch - 1)
        def _():
            st_ref[i // tiles_per_batch, :, pl.ds(col0, COL_BLOCK)] = ext_s[0, tb + 1:tb + POOL_HEAD, :]

        carry_s[g] = ext_s[0, tb:tb + POOL_HEAD, :]

    mixed = jnp.dot(pooled_s[...].astype(BF16), wgrp_ref[0], preferred_element_type=F32) * sc_ref[...]
    gate = jnp.dot(h_s[...], wg_ref[...], preferred_element_type=F32)
    z_ref[...] = (mixed * _silu(gate)).astype(BF16)


def _pool_in(x2d, norm_g, w_in, w_grp, scale, state, *, n_batch, seq, tm):
    m = x2d.shape[0]
    has_state = state is not None
    if has_state:
        assert tm == m
        nb, tb, tiles_per_batch = n_batch, seq, 1
    else:
        assert seq % tm == 0 and tm >= POOL_HEAD
        nb, tb, tiles_per_batch = 1, tm, seq // tm
    n_groups = len(POOL_WINDOWS)
    in_specs = [
        pl.BlockSpec((tm, D_MODEL), lambda i, g: (i, 0)),
        pl.BlockSpec((1, D_MODEL), lambda i, g: (0, 0)),
        pl.BlockSpec((D_MODEL, COL_BLOCK), lambda i, g: (0, g)),
        pl.BlockSpec((D_MODEL, COL_BLOCK), lambda i, g: (0, n_groups + g)),
        pl.BlockSpec((1, POOL_GROUP, POOL_GROUP), lambda i, g: (g, 0, 0)),
        pl.BlockSpec((1, COL_BLOCK), lambda i, g: (0, g)),
    ]
    args = [x2d, norm_g.reshape(1, D_MODEL), w_in, w_in, w_grp, scale.reshape(1, D_INNER)]
    scratch = [
        pltpu.VMEM((tm, D_MODEL), BF16),
        pltpu.VMEM((nb, POOL_HEAD + tb, COL_BLOCK), F32),
        pltpu.VMEM((tm, COL_BLOCK), F32),
    ]
    if has_state:
        in_specs.append(pl.BlockSpec((nb, POOL_STATE, COL_BLOCK), lambda i, g: (0, 0, g)))
        args.append(state)
    else:
        scratch.append(pltpu.VMEM((n_groups, POOL_HEAD, COL_BLOCK), F32))
    kern = functools.partial(_pool_in_kernel, nb=nb, tb=tb, tiles_per_batch=tiles_per_batch,
                             has_state=has_state)
    return pl.pallas_call(
        kern,
        grid=(m // tm, n_groups),
        in_specs=in_specs,
        out_specs=[
            pl.BlockSpec((tm, COL_BLOCK), lambda i, g: (i, g)),
            pl.BlockSpec((n_batch, POOL_STATE, D_INNER), lambda i, g: (0, 0, 0)),
        ],
        out_shape=[
            jax.ShapeDtypeStruct((m, D_INNER), BF16),
            jax.ShapeDtypeStruct((n_batch, POOL_STATE, D_INNER), F32),
        ],
        scratch_shapes=scratch,
        compiler_params=_params(("arbitrary", "arbitrary")),
        name="pool_in",
    )(*args)


def _out_proj_kernel(z_ref, w_ref, x_ref, o_ref):
    o_ref[...] = x_ref[...] + jnp.dot(z_ref[...], w_ref[...], preferred_element_type=F32)


def _out_proj(z, w, x2d, *, tm, tn):
    m, k = z.shape
    n = w.shape[1]
    return pl.pallas_call(
        _out_proj_kernel,
        grid=(m // tm, n // tn),
        in_specs=[
            pl.BlockSpec((tm, k), lambda i, j: (i, 0)),
            pl.BlockSpec((k, tn), lambda i, j: (0, j)),
            pl.BlockSpec((tm, tn), lambda i, j: (i, j)),
        ],
        out_specs=pl.BlockSpec((tm, tn), lambda i, j: (i, j)),
        out_shape=jax.ShapeDtypeStruct((m, n), F32),
        compiler_params=_params(("arbitrary", "arbitrary")),
        name="out_proj",
    )(z, w, x2d)


def _ln_out_proj_kernel(y_ref, gate_ref, lg_ref, lb_ref, w_ref, x_ref, o_ref, z_s):
    @pl.when(pl.program_id(1) == 0)
    def _():
        def rows(it, carry):
            r0 = pl.multiple_of(it * BF16_ROWS, BF16_ROWS)
            y = y_ref[pl.ds(r0, BF16_ROWS), :]
            yc = y - jnp.mean(y, axis=-1, keepdims=True)
            var = jnp.mean(yc * yc, axis=-1, keepdims=True)
            ln = yc * lax.rsqrt(var + NORM_EPS) * lg_ref[...] + lb_ref[...]
            gate = gate_ref[pl.ds(r0, BF16_ROWS), :].astype(F32)
            z_s[pl.ds(r0, BF16_ROWS), :] = (_silu(ln) * _silu(gate)).astype(BF16)
            return carry

        lax.fori_loop(0, z_s.shape[0] // BF16_ROWS, rows, 0)

    o_ref[...] = x_ref[...] + jnp.dot(z_s[...], w_ref[...], preferred_element_type=F32)


def _ln_out_proj(y, gate, ln_g, ln_b, w, x2d, *, tm, tn):
    m, k = y.shape
    n = w.shape[1]
    return pl.pallas_call(
        _ln_out_proj_kernel,
        grid=(m // tm, n // tn),
        in_specs=[
            pl.BlockSpec((tm, k), lambda i, j: (i, 0)),
            pl.BlockSpec((tm, k), lambda i, j: (i, 0)),
            pl.BlockSpec((1, k), lambda i, j: (0, 0)),
            pl.BlockSpec((1, k), lambda i, j: (0, 0)),
            pl.BlockSpec((k, tn), lambda i, j: (0, j)),
            pl.BlockSpec((tm, tn), lambda i, j: (i, j)),
        ],
        out_specs=pl.BlockSpec((tm, tn), lambda i, j: (i, j)),
        out_shape=jax.ShapeDtypeStruct((m, n), F32),
        scratch_shapes=[pltpu.VMEM((tm, k), BF16)],
        compiler_params=_params(("arbitrary", "arbitrary")),
        name="ln_out_proj",
    )(y, gate, ln_g.reshape(1, k), ln_b.reshape(1, k), w, x2d)


def _head_norm(y, g128):
    lane = lax.broadcasted_iota(jnp.int32, (1, LANE), 1)
    low = lane < ATT_HEAD_DIM
    outs = []
    for c in range(y.shape[1] // LANE):
        blk = y[:, c * LANE:(c + 1) * LANE]
        sq = blk * blk
        s_lo = jnp.sum(jnp.where(low, sq, 0.0), axis=-1, keepdims=True)
        s_hi = jnp.sum(jnp.where(low, 0.0, sq), axis=-1, keepdims=True)
        ms = jnp.where(low, s_lo, s_hi) * (1.0 / ATT_HEAD_DIM)
        outs.append(blk * lax.rsqrt(ms + NORM_EPS) * g128)
    return jnp.concatenate(outs, axis=1)


def _attn_in_kernel(x_ref, ng_ref, w_ref, wkt_ref, qg_ref, kg_ref, q_ref, kt_ref, v_ref, vn_ref, gate_ref, h_s,
                    *, q_dtype, v_dtype, tm):
    j = pl.program_id(1)

    @pl.when(j == 0)
    def _():
        _rms_rows(x_ref, ng_ref, h_s)

    @pl.when(j < 2)
    def _():
        y = jnp.dot(h_s[...], w_ref[...], preferred_element_type=F32)
        q_ref[...] = (_head_norm(y, qg_ref[...]) * (ATT_HEAD_DIM ** -0.5)).astype(q_dtype)

    @pl.when(jnp.logical_and(j >= 2, j < 4))
    def _():
        yt = lax.dot_general(wkt_ref[...], h_s[...], (((1,), (1,)), ((), ())), preferred_element_type=F32)
        y3 = yt.reshape(COL_BLOCK // ATT_HEAD_DIM, ATT_HEAD_DIM, tm)
        ms = jnp.mean(y3 * y3, axis=1, keepdims=True)
        kt_ref[...] = (y3 * lax.rsqrt(ms + NORM_EPS) * kg_ref[...][None]).reshape(COL_BLOCK, tm)

    for jj in (4, 5):
        @pl.when(j == jj)
        def _(jj=jj):
            y = jnp.dot(h_s[...], w_ref[...], preferred_element_type=F32)
            v_ref[...] = y.astype(v_dtype)
            if vn_ref is not None:
                for hl in range(COL_BLOCK // LANE):
                    hh = (jj - 4) * (COL_BLOCK // LANE) + hl
                    vn_ref[pl.ds(hh, tm, stride=ATT_HEADS), :] = y[:, hl * LANE:(hl + 1) * LANE]

    @pl.when(j >= 6)
    def _():
        gate_ref[...] = jnp.dot(h_s[...], w_ref[...], preferred_element_type=F32)


def _attn_in(x2d, norm_g, w_in, w_kt, q_norm_g, k_norm_g, *, n_batch, seq, tm, q_dtype, v_dtype, native_v):
    m = x2d.shape[0]
    tiles_per_batch = seq // tm
    n_col = w_in.shape[1] // COL_BLOCK

    def out_spec(first):
        return pl.BlockSpec((tm, COL_BLOCK), lambda i, j: (i, jnp.clip(j - first, 0, 1)))

    out_specs = [
        out_spec(0),
        pl.BlockSpec((None, COL_BLOCK, tm),
                     lambda i, j: (i // tiles_per_batch, jnp.clip(j - 2, 0, 1), i % tiles_per_batch)),
        out_spec(4),
    ]
    out_shape = [
        jax.ShapeDtypeStruct((m, D_MODEL), q_dtype),
        jax.ShapeDtypeStruct((n_batch, D_MODEL, seq), F32),
        jax.ShapeDtypeStruct((m, D_MODEL), v_dtype),
    ]
    if native_v:
        out_specs.append(pl.BlockSpec((tm * ATT_HEADS, LANE), lambda i, j: (i, 0)))
        out_shape.append(jax.ShapeDtypeStruct((m * ATT_HEADS, LANE), F32))
    out_specs.append(out_spec(6))
    out_shape.append(jax.ShapeDtypeStruct((m, D_MODEL), F32))

    def kern(*refs):
        ins, outs, scratch = refs[:6], refs[6:-1], refs[-1]
        if native_v:
            q_ref, kt_ref, v_ref, vn_ref, gate_ref = outs
        else:
            (q_ref, kt_ref, v_ref, gate_ref), vn_ref = outs, None
        _attn_in_kernel(*ins, q_ref, kt_ref, v_ref, vn_ref, gate_ref, scratch,
                        q_dtype=q_dtype, v_dtype=v_dtype, tm=tm)

    return pl.pallas_call(
        kern,
        grid=(m // tm, n_col),
        in_specs=[
            pl.BlockSpec((tm, D_MODEL), lambda i, j: (i, 0)),
            pl.BlockSpec((1, D_MODEL), lambda i, j: (0, 0)),
            pl.BlockSpec((D_MODEL, COL_BLOCK), lambda i, j: (0, jnp.where(jnp.logical_and(j >= 2, j < 4), 1, j))),
            pl.BlockSpec((COL_BLOCK, D_MODEL), lambda i, j: (jnp.clip(j - 2, 0, 1), 0)),
            pl.BlockSpec((1, LANE), lambda i, j: (0, 0)),
            pl.BlockSpec((ATT_HEAD_DIM, 1), lambda i, j: (0, 0)),
        ],
        out_specs=out_specs,
        out_shape=out_shape,
        scratch_shapes=[pltpu.VMEM((tm, D_MODEL), BF16)],
        compiler_params=_params(("arbitrary", "arbitrary")),
        name="attn_in",
    )(x2d, norm_g.reshape(1, D_MODEL), w_in, w_kt, jnp.tile(q_norm_g, 2).reshape(1, LANE),
      k_norm_g.reshape(ATT_HEAD_DIM, 1))


def _lambda(lq1_ref, lk1_ref, lq2_ref, lk2_ref):
    a = jnp.sum(lq1_ref[...] * lk1_ref[...], axis=-1, keepdims=True)
    b = jnp.sum(lq2_ref[...] * lk2_ref[...], axis=-1, keepdims=True)
    return jnp.exp(a) - jnp.exp(b) + LAMBDA_INIT


def _sub_ln_gate(o, subln_g, gate):
    ms = jnp.mean(o * o, axis=-1, keepdims=True)
    o = o * lax.rsqrt(ms + NORM_EPS) * subln_g * (1.0 - LAMBDA_INIT)
    return o * _silu(gate)


def _attn_prompt_kernel(q_ref, kt_ref, v_ref, gate_ref, sg_ref, lq1_ref, lk1_ref, lq2_ref, lk2_ref,
                        z_ref, ktb_s, m_s, l_s, acc_s, *, tq, tk):
    h = pl.program_id(1)
    qi = pl.program_id(2)
    rep = tk // LANE

    @pl.when(qi == 0)
    def _():
        ktb_s[...] = kt_ref[...].astype(BF16)

    q = q_ref[...]
    lane = lax.broadcasted_iota(jnp.int32, (1, LANE), 1)
    low = lane < ATT_HEAD_DIM
    zero = jnp.zeros_like(q)
    qc = (jnp.where(low, q, zero), jnp.where(low, zero, q))

    slope = jnp.exp2(-0.5 * jnp.full((1, tk), h + 1, jnp.int32).astype(F32))
    col = lax.broadcasted_iota(jnp.int32, (1, tk), 1)

    m_s[...] = jnp.full(m_s.shape, NEG_INF, F32)
    l_s[...] = jnp.zeros(l_s.shape, F32)
    acc_s[...] = jnp.zeros(acc_s.shape, F32)

    def tile(kj, masked):
        start = pl.multiple_of(kj * tk, tk)
        kt = ktb_s[:, pl.ds(start, tk)]
        vt = v_ref[pl.ds(start, tk), :]
        bias = slope * (col + (kj * tk - qi * tq)).astype(F32)
        for c in range(2):
            s = jnp.dot(qc[c], kt, preferred_element_type=F32) + bias
            if masked:
                row_i = lax.broadcasted_iota(jnp.int32, (tq, tk), 0)
                col_j = lax.broadcasted_iota(jnp.int32, (tq, tk), 1) + (kj * tk - qi * tq)
                s = jnp.where(col_j <= row_i, s, NEG_INF)
            m_old = m_s[c]
            m_new = jnp.maximum(m_old, jnp.max(s, axis=-1, keepdims=True))
            alpha = jnp.exp(m_old - m_new)
            p = jnp.exp(s - pltpu.repeat(m_new, rep, axis=1))
            l_s[c] = alpha * l_s[c] + jnp.sum(p, axis=-1, keepdims=True)
            acc_s[c] = alpha * acc_s[c] + jnp.dot(p.astype(BF16), vt, preferred_element_type=F32)
            m_s[c] = m_new

    n_full = (qi * tq) // tk
    n_diag = tq // tk

    def body(kj, carry):
        tile(kj, False)
        return carry

    lax.fori_loop(0, n_full, body, 0)
    for d in range(n_diag):
        tile(n_full + d, True)

    lam = _lambda(lq1_ref, lk1_ref, lq2_ref, lk2_ref)
    o = acc_s[0] / l_s[0] - lam * (acc_s[1] / l_s[1])
    z_ref[...] = _sub_ln_gate(o, sg_ref[...], gate_ref[...]).astype(BF16)


def _attn_prompt(q, kt, v, gate, subln_g, lams, *, n_batch, seq, tq, tk):
    vec = pl.BlockSpec((1, ATT_HEAD_DIM), lambda b, h, i: (0, 0))
    kern = functools.partial(_attn_prompt_kernel, tq=tq, tk=tk)
    return pl.pallas_call(
        kern,
        grid=(n_batch, ATT_HEADS, seq // tq),
        in_specs=[
            pl.BlockSpec((None, tq, LANE), lambda b, h, i: (b, i, h)),
            pl.BlockSpec((None, LANE, seq), lambda b, h, i: (b, h, 0)),
            pl.BlockSpec((None, seq, LANE), lambda b, h, i: (b, 0, h)),
            pl.BlockSpec((None, tq, LANE), lambda b, h, i: (b, i, h)),
            pl.BlockSpec((1, LANE), lambda b, h, i: (0, 0)),
            vec, vec, vec, vec,
        ],
        out_specs=pl.BlockSpec((None, tq, LANE), lambda b, h, i: (b, i, h)),
        out_shape=jax.ShapeDtypeStruct((n_batch, seq, D_MODEL), BF16),
        scratch_shapes=[
            pltpu.VMEM((LANE, seq), BF16),
            pltpu.VMEM((2, tq, LANE), F32),
            pltpu.VMEM((2, tq, LANE), F32),
            pltpu.VMEM((2, tq, LANE), F32),
        ],
        compiler_params=_params(("arbitrary", "arbitrary", "arbitrary")),
        name="attn_prompt",
    )(q, kt, v, gate, subln_g.reshape(1, LANE), *[t.reshape(1, ATT_HEAD_DIM) for t in lams])


def _attn_sample_kernel(pt_ref, q_ref, knt_ref, vn_ref, gate_ref, sg_ref,
                        lq1_ref, lk1_ref, lq2_ref, lk2_ref, *rest, pages, n_steps, past_len, t_new):
    k_refs = rest[:pages]
    v_refs = rest[pages:2 * pages]
    z_ref, q3_s, vh_s, slope_s, m_s, l_s, acc_s = rest[2 * pages:]
    b = pl.program_id(0)
    p = pl.program_id(1)
    n_rows = 2 * t_new
    shape3 = (ATT_HEADS, n_rows, LANE)

    @pl.when(p == 0)
    def _():
        row_c = lax.broadcasted_iota(jnp.int32, (n_rows, LANE), 0) // t_new
        lane_c = lax.broadcasted_iota(jnp.int32, (n_rows, LANE), 1) // ATT_HEAD_DIM
        for hh in range(ATT_HEADS):
            blk = q_ref[:, hh * LANE:(hh + 1) * LANE]
            both = jnp.concatenate([blk, blk], axis=0)
            q3_s[hh] = jnp.where(row_c == lane_c, both, 0.0).astype(BF16)
        head = lax.broadcasted_iota(jnp.int32, shape3, 0)
        slope_s[...] = jnp.exp2(-0.5 * (head + 1).astype(F32))
        m_s[...] = jnp.full(shape3, NEG_INF, F32)
        l_s[...] = jnp.zeros(shape3, F32)
        acc_s[...] = jnp.zeros(shape3, F32)

    q3 = q3_s[...]

    def update(s, vb):
        m_old = m_s[...]
        m_new = jnp.maximum(m_old, jnp.max(s, axis=-1, keepdims=True))
        alpha = jnp.exp(m_old - m_new)
        t = s.shape[-1]
        m_all = pltpu.repeat(m_new, t // LANE, axis=2) if t % LANE == 0 else m_new[:, :, 0:t]
        pr = jnp.exp(s - m_all)
        l_s[...] = alpha * l_s[...] + jnp.sum(pr, axis=-1, keepdims=True)
        pv = jnp.einsum("hrt,hte->hre", pr.astype(BF16), vb, preferred_element_type=F32)
        acc_s[...] = alpha * acc_s[...] + pv
        m_s[...] = m_new

    tok = lax.broadcasted_iota(jnp.int32, (1, 1, PAGE_SIZE), 2)
    slope = slope_s[...]
    scores = []
    for r in range(pages):
        k3 = k_refs[r][...].reshape(ATT_HEADS, LANE, PAGE_SIZE).astype(BF16)
        k_pos = ((p * pages + r) * PAGE_SIZE - past_len + tok).astype(F32)
        scores.append(jnp.einsum("hrk,hkt->hrt", q3, k3, preferred_element_type=F32) + slope * k_pos)
        for hh in range(ATT_HEADS):
            vh_s[hh, r * PAGE_SIZE:(r + 1) * PAGE_SIZE, :] = (
                v_refs[r][pl.ds(hh, PAGE_SIZE, stride=ATT_HEADS), :].astype(BF16))
    update(jnp.concatenate(scores, axis=-1), vh_s[...])

    @pl.when(p == n_steps - 1)
    def _():
        n_tok = knt_ref.shape[1]
        k3 = knt_ref[...].reshape(ATT_HEADS, LANE, n_tok).astype(BF16)
        tok_n = lax.broadcasted_iota(jnp.int32, (1, n_rows, n_tok), 2)
        row_t = lax.broadcasted_iota(jnp.int32, (1, n_rows, n_tok), 1) % t_new
        valid = jnp.logical_and(tok_n // t_new == b, tok_n % t_new <= row_t)
        s = jnp.einsum("hrk,hkt->hrt", q3, k3, preferred_element_type=F32)
        s = s + slope[:, :, 0:n_tok] * (tok_n % t_new).astype(F32)
        s = jnp.where(valid, s, NEG_INF)
        vb = jnp.stack([vn_ref[:, hh * LANE:(hh + 1) * LANE] for hh in range(ATT_HEADS)], axis=0).astype(BF16)
        update(s, vb)

        lam = _lambda(lq1_ref, lk1_ref, lq2_ref, lk2_ref)
        acc = acc_s[...]
        l = l_s[...]
        o = acc[:, 0:t_new, :] / l[:, 0:t_new, :] - lam * (acc[:, t_new:, :] / l[:, t_new:, :])
        for hh in range(ATT_HEADS):
            gate = gate_ref[:, hh * LANE:(hh + 1) * LANE]
            z_ref[:, hh * LANE:(hh + 1) * LANE] = _sub_ln_gate(o[hh], sg_ref[...], gate)


def _attn_sample(q, knt, v_new, gate, cache_kt, cache_v2, page_table, subln_g, lams, *, pages):
    n_batch, t_new, _ = q.shape
    n_pages = page_table.shape[1]
    n_steps = n_pages // pages
    n_rows = 2 * t_new
    n_tok = n_batch * t_new

    tok_spec = pl.BlockSpec((None, t_new, D_MODEL), lambda b, p, pt: (b, 0, 0))
    vec = pl.BlockSpec((1, ATT_HEAD_DIM), lambda b, p, pt: (0, 0))

    def page_spec(r):
        return pl.BlockSpec((None, D_MODEL, PAGE_SIZE), lambda b, p, pt: (pt[b, p * pages + r], 0, 0))

    kern = functools.partial(_attn_sample_kernel, pages=pages, n_steps=n_steps,
                             past_len=n_pages * PAGE_SIZE, t_new=t_new)
    grid_spec = pltpu.PrefetchScalarGridSpec(
        num_scalar_prefetch=1,
        grid=(n_batch, n_steps),
        in_specs=[tok_spec,
                  pl.BlockSpec((D_MODEL, n_tok), lambda b, p, pt: (0, 0)),
                  pl.BlockSpec((n_tok, D_MODEL), lambda b, p, pt: (0, 0)),
                  tok_spec,
                  pl.BlockSpec((1, LANE), lambda b, p, pt: (0, 0)), vec, vec, vec, vec]
        + [page_spec(r) for r in range(pages)] + [page_spec(r) for r in range(pages)],
        out_specs=tok_spec,
        scratch_shapes=[
            pltpu.VMEM((ATT_HEADS, n_rows, LANE), BF16),
            pltpu.VMEM((ATT_HEADS, pages * PAGE_SIZE, LANE), BF16),
            pltpu.VMEM((ATT_HEADS, n_rows, LANE), F32),
            pltpu.VMEM((ATT_HEADS, n_rows, LANE), F32),
            pltpu.VMEM((ATT_HEADS, n_rows, LANE), F32),
            pltpu.VMEM((ATT_HEADS, n_rows, LANE), F32),
        ],
    )
    return pl.pallas_call(
        kern,
        grid_spec=grid_spec,
        out_shape=jax.ShapeDtypeStruct((n_batch, t_new, D_MODEL), F32),
        compiler_params=_params(("arbitrary", "arbitrary")),
        name="attn_sample",
    )(page_table, q, knt, v_new, gate, subln_g.reshape(1, LANE),
      *[t.reshape(1, ATT_HEAD_DIM) for t in lams], *([cache_kt] * pages), *([cache_v2] * pages))


def _conv_in_kernel(*refs, nb, tb, tiles_per_batch, has_state):
    if has_state:
        (x_ref, ng_ref, wa_ref, wl_ref, wg_ref, cw_ref, cb_ref, st_in_ref,
         y_ref, gate_ref, st_ref, h_s, ext_s, wb_s) = refs
        carry_s = None
    else:
        (x_ref, ng_ref, wa_ref, wl_ref, wg_ref, cw_ref, cb_ref,
         y_ref, gate_ref, st_ref, h_s, ext_s, wb_s, carry_s) = refs
    i = pl.program_id(0)
    cb = pl.program_id(1)

    @pl.when(cb == 0)
    def _():
        _rms_rows(x_ref, ng_ref, h_s)

    a = jnp.dot(h_s[...], wa_ref[...], preferred_element_type=F32)
    glu = jnp.dot(h_s[...], wl_ref[...], preferred_element_type=F32)
    c = a * jax.nn.sigmoid(glu)

    head0 = CONV_HEAD - CONV_STATE
    if has_state:
        ext_s[:, 0:head0, :] = jnp.zeros((nb, head0, COL_BLOCK), F32)
        ext_s[:, head0:CONV_HEAD, :] = st_in_ref[...]
    else:
        first = (i % tiles_per_batch) == 0

        @pl.when(first)
        def _():
            ext_s[0, 0:CONV_HEAD, :] = jnp.zeros((CONV_HEAD, COL_BLOCK), F32)

        @pl.when(jnp.logical_not(first))
        def _():
            ext_s[0, 0:CONV_HEAD, :] = carry_s[cb]

    for b in range(nb):
        ext_s[b, CONV_HEAD:CONV_HEAD + tb, :] = c[b * tb:(b + 1) * tb, :]

    rc = min(tb, ROW_CHUNK)
    n_lane = COL_BLOCK // LANE_CHUNK
    n_row = tb // rc

    for j in range(CONV_WIDTH):
        wb_s[j] = jnp.broadcast_to(cw_ref[j:j + 1, :], (SUBLANES, COL_BLOCK))

    def chunk(it, carry):
        b = it // (n_row * n_lane)
        r0 = pl.multiple_of(((it // n_lane) % n_row) * rc, rc)
        l0 = pl.multiple_of((it % n_lane) * LANE_CHUNK, LANE_CHUNK)
        blk = ext_s[b, pl.ds(r0, rc + CONV_HEAD), pl.ds(l0, LANE_CHUNK)]
        acc = jnp.broadcast_to(cb_ref[:, pl.ds(l0, LANE_CHUNK)], (rc, LANE_CHUNK))
        for r in range(SUBLANES):
            shifted = blk if r == 0 else blk[r:r + rc + CONV_HEAD - SUBLANES, :]
            for j in range(CONV_WIDTH):
                if (head0 + j) % SUBLANES == r:
                    a = head0 + j - r
                    w_j = pltpu.repeat(wb_s[j, :, pl.ds(l0, LANE_CHUNK)], rc // SUBLANES, axis=0)
                    acc = acc + w_j * shifted[a:a + rc, :]
        y_ref[pl.ds(b * tb + r0, rc), pl.ds(l0, LANE_CHUNK)] = acc
        return carry

    lax.fori_loop(0, nb * n_row * n_lane, chunk, 0)
    col0 = pl.multiple_of(cb * COL_BLOCK, COL_BLOCK)
    if has_state:
        for b in range(nb):
            st_ref[b, :, pl.ds(col0, COL_BLOCK)] = ext_s[b, tb + head0:tb + CONV_HEAD, :]
    else:
        @pl.when((i % tiles_per_batch) == tiles_per_batch - 1)
        def _():
            st_ref[i // tiles_per_batch, :, pl.ds(col0, COL_BLOCK)] = ext_s[0, tb + head0:tb + CONV_HEAD, :]

        carry_s[cb] = ext_s[0, tb:tb + CONV_HEAD, :]

    gate_ref[...] = jnp.dot(h_s[...], wg_ref[...], preferred_element_type=F32).astype(BF16)


def _conv_in(x2d, norm_g, w_in, conv_w, conv_b, state, *, n_batch, seq, tm):
    m = x2d.shape[0]
    has_state = state is not None
    if has_state:
        assert tm == m
        nb, tb, tiles_per_batch = n_batch, seq, 1
    else:
        assert seq % tm == 0 and tm >= CONV_HEAD
        nb, tb, tiles_per_batch = 1, tm, seq // tm
    n_cb = D_INNER // COL_BLOCK
    in_specs = [
        pl.BlockSpec((tm, D_MODEL), lambda i, c: (i, 0)),
        pl.BlockSpec((1, D_MODEL), lambda i, c: (0, 0)),
        pl.BlockSpec((D_MODEL, COL_BLOCK), lambda i, c: (0, c)),
        pl.BlockSpec((D_MODEL, COL_BLOCK), lambda i, c: (0, n_cb + c)),
        pl.BlockSpec((D_MODEL, COL_BLOCK), lambda i, c: (0, 2 * n_cb + c)),
        pl.BlockSpec((CONV_WIDTH, COL_BLOCK), lambda i, c: (0, c)),
        pl.BlockSpec((1, COL_BLOCK), lambda i, c: (0, c)),
    ]
    args = [x2d, norm_g.reshape(1, D_MODEL), w_in, w_in, w_in, conv_w, conv_b.reshape(1, D_INNER)]
    scratch = [
        pltpu.VMEM((tm, D_MODEL), BF16),
        pltpu.VMEM((nb, CONV_HEAD + tb, COL_BLOCK), F32),
        pltpu.VMEM((CONV_WIDTH, SUBLANES, COL_BLOCK), F32),
    ]
    if has_state:
        in_specs.append(pl.BlockSpec((nb, CONV_STATE, COL_BLOCK), lambda i, c: (0, 0, c)))
        args.append(state)
    else:
        scratch.append(pltpu.VMEM((n_cb, CONV_HEAD, COL_BLOCK), F32))
    kern = functools.partial(_conv_in_kernel, nb=nb, tb=tb, tiles_per_batch=tiles_per_batch,
                             has_state=has_state)
    return pl.pallas_call(
        kern,
        grid=(m // tm, n_cb),
        in_specs=in_specs,
        out_specs=[
            pl.BlockSpec((tm, COL_BLOCK), lambda i, c: (i, c)),
            pl.BlockSpec((tm, COL_BLOCK), lambda i, c: (i, c)),
            pl.BlockSpec((n_batch, CONV_STATE, D_INNER), lambda i, c: (0, 0, 0)),
        ],
        out_shape=[
            jax.ShapeDtypeStruct((m, D_INNER), F32),
            jax.ShapeDtypeStruct((m, D_INNER), BF16),
            jax.ShapeDtypeStruct((n_batch, CONV_STATE, D_INNER), F32),
        ],
        scratch_shapes=scratch,
        compiler_params=_params(("arbitrary", "arbitrary")),
        name="conv_in",
    )(*args)


PROMPT_TM = 512
LN_TM = 512
LN_TN = 512
ATTN_TQ = 512
ATTN_TK = 512
SAMPLE_PAGES_PER_STEP = 4


def kernel(x_prompt, x_sample, state_pool_l0, cache_k_l1, cache_v_l1, state_conv_l2, state_pool_l3, page_table, norm_g_l0, w_in_l0, w_grp_l0, pool_scale_l0, w_out_l0, norm_g_l1, w_in_l1, q_norm_g_l1, k_norm_g_l1, lambda_q1_l1, lambda_k1_l1, lambda_q2_l1, lambda_k2_l1, subln_g_l1, w_out_l1, norm_g_l2, w_in_l2, conv_w_l2, conv_b_l2, ln_g_l2, ln_b_l2, w_out_l2, norm_g_l3, w_in_l3, w_grp_l3, pool_scale_l3, w_out_l3):
    nbp, seq, _ = x_prompt.shape
    nbs, t_new, _ = x_sample.shape
    mp, ms = nbp * seq, nbs * t_new
    xp = x_prompt.reshape(mp, D_MODEL)
    xs = x_sample.reshape(ms, D_MODEL)
    lams = (lambda_q1_l1, lambda_k1_l1, lambda_q2_l1, lambda_k2_l1)

    def bf(w):
        return w.astype(BF16)

    def pool_layer(xp, xs, state, norm_g, w_in, w_grp, scale, w_out):
        w_in, w_grp, w_out = bf(w_in), bf(w_grp), bf(w_out)
        zp, stp = _pool_in(xp, norm_g, w_in, w_grp, scale, None, n_batch=nbp, seq=seq, tm=PROMPT_TM)
        zs, sts = _pool_in(xs, norm_g, w_in, w_grp, scale, state, n_batch=nbs, seq=t_new, tm=ms)
        xp = _out_proj(zp, w_out, xp, tm=PROMPT_TM, tn=COL_BLOCK)
        xs = _out_proj(zs, w_out, xs, tm=ms, tn=COL_BLOCK)
        return xp, xs, stp, sts

    xp, xs, pool0_p, pool0_s = pool_layer(xp, xs, state_pool_l0, norm_g_l0, w_in_l0, w_grp_l0,
                                          pool_scale_l0, w_out_l0)

    w_in, w_out = bf(w_in_l1), bf(w_out_l1)
    w_kt = bf(lax.slice(w_in_l1, (0, D_MODEL), (D_MODEL, 2 * D_MODEL))).T
    qp, ktp, vp, vnp, gp = _attn_in(xp, norm_g_l1, w_in, w_kt, q_norm_g_l1, k_norm_g_l1, n_batch=nbp, seq=seq,
                                    tm=PROMPT_TM, q_dtype=BF16, v_dtype=BF16, native_v=True)
    qs, kts, vs, gs = _attn_in(xs, norm_g_l1, w_in, w_kt, q_norm_g_l1, k_norm_g_l1, n_batch=1, seq=ms,
                               tm=ms, q_dtype=F32, v_dtype=F32, native_v=False)
    shp = (nbp, seq, D_MODEL)
    zp = _attn_prompt(qp.reshape(shp), ktp, vp.reshape(shp), gp.reshape(shp), subln_g_l1, lams,
                      n_batch=nbp, seq=seq, tq=ATTN_TQ, tk=ATTN_TK)
    n_phys = cache_k_l1.shape[0]
    cache_kt = jnp.transpose(cache_k_l1, (0, 2, 3, 4, 1)).reshape(n_phys, D_MODEL, PAGE_SIZE)
    cache_v2 = cache_v_l1.reshape(n_phys, PAGE_SIZE * ATT_HEADS, ATT_VDIM)
    shs = (nbs, t_new, D_MODEL)
    zs = _attn_sample(qs.reshape(shs), kts[0], vs, gs.reshape(shs), cache_kt, cache_v2, page_table,
                      subln_g_l1, lams, pages=SAMPLE_PAGES_PER_STEP)
    xp = _out_proj(zp.reshape(mp, D_MODEL), w_out, xp, tm=PROMPT_TM, tn=COL_BLOCK)
    xs = _out_proj(zs.reshape(ms, D_MODEL).astype(BF16), w_out, xs, tm=ms, tn=COL_BLOCK)
    new_k_p = jnp.transpose(ktp.reshape(nbp, ATT_HEADS, 2, ATT_HEAD_DIM, seq), (0, 4, 1, 2, 3))
    new_v_p = vnp.reshape(nbp, seq, ATT_HEADS, ATT_VDIM)
    new_k_s = kts[0].T.reshape(nbs, t_new, ATT_HEADS, 2, ATT_HEAD_DIM)
    new_v_s = vs.reshape(nbs, t_new, ATT_HEADS, ATT_VDIM)

    w_in, w_out = bf(w_in_l2), bf(w_out_l2)
    yp, gp, conv_p = _conv_in(xp, norm_g_l2, w_in, conv_w_l2, conv_b_l2, None, n_batch=nbp, seq=seq,
                              tm=PROMPT_TM)
    ys, gs, conv_s = _conv_in(xs, norm_g_l2, w_in, conv_w_l2, conv_b_l2, state_conv_l2, n_batch=nbs,
                              seq=t_new, tm=ms)
    xp = _ln_out_proj(yp, gp, ln_g_l2, ln_b_l2, w_out, xp, tm=LN_TM, tn=LN_TN)
    xs = _ln_out_proj(ys, gs, ln_g_l2, ln_b_l2, w_out, xs, tm=ms, tn=LN_TN)

    xp, xs, pool3_p, pool3_s = pool_layer(xp, xs, state_pool_l3, norm_g_l3, w_in_l3, w_grp_l3,
                                          pool_scale_l3, w_out_l3)

    return (xp.reshape(nbp, seq, D_MODEL), xs.reshape(nbs, t_new, D_MODEL), pool0_p, pool0_s,
            new_k_p, new_v_p, new_k_s, new_v_s, conv_p, conv_s, pool3_p, pool3_s)
```

```python
import functools
import math

import jax
import jax.numpy as jnp
from jax import lax
from jax.experimental import pallas as pl
from jax.experimental.pallas import tpu as pltpu

F32 = jnp.float32
BF16 = jnp.bfloat16

D_MODEL = 2048
D_INNER = 4096
POOL_WINDOWS = (2, 4, 8, 16)
POOL_GROUP = 1024
POOL_STATE = 15
ATT_HEADS = 16
ATT_HEAD_DIM = 64
ATT_VDIM = 128
LAMBDA_INIT = 0.8 - 0.6 * math.exp(-0.3 * 1)
LOG2E = math.log2(math.e)
QK_SCALE = ATT_HEAD_DIM ** -0.5 * LOG2E
CONV_WIDTH = 31
CONV_STATE = 30
NORM_EPS = 1e-6
NEG_INF = -1e30
PAGE_SIZE = 128

VMEM_LIMIT_BYTES = 58 * 1024 * 1024
LANE = 128
SUBLANES = 8
BF16_ROWS = 16
COL_BLOCK = 1024
POOL_HEAD = 16
CONV_HEAD = 32
ROW_CHUNK = 32
LANE_CHUNK = 256


def _params(sem):
    return pltpu.CompilerParams(dimension_semantics=sem, vmem_limit_bytes=VMEM_LIMIT_BYTES)


def _rms_rows(x_ref, g_ref, h_ref):
    xf = x_ref[...]
    ms = jnp.mean(xf * xf, axis=-1, keepdims=True)
    h_ref[...] = (xf * lax.rsqrt(ms + NORM_EPS) * g_ref[...]).astype(BF16)


def _silu(x):
    return x * jax.nn.sigmoid(x)


def _pool_rows(blk, w, n_seen):
    rows = blk.shape[0] - POOL_HEAD
    cur = blk[POOL_HEAD:, :]
    acc = cur
    for j in range(1, w):
        acc = acc + blk[POOL_HEAD - j:POOL_HEAD - j + rows, :]
    cnt = jnp.minimum(n_seen, w).astype(F32)
    return jnp.where(n_seen >= w, acc * (1.0 / w), acc / cnt) - cur


def _pool_finish(h_s, pooled, wg_ref, wgrp_ref, sc_ref, z_ref):
    mixed = jnp.dot(pooled, wgrp_ref[0], preferred_element_type=F32) * sc_ref[...]
    gate = jnp.dot(h_s[...], wg_ref[...], preferred_element_type=F32)
    z_ref[...] = (mixed * _silu(gate)).astype(BF16)


def _pool_in_prompt_kernel(x_ref, ng_ref, wu_ref, wg_ref, wgrp_ref, sc_ref, pmat_ref,
                           z_ref, st_ref, h_s, edge_s, pooled_s, carry_s, *, tm, tiles_per_batch):
    i = pl.program_id(0)
    g = pl.program_id(1)

    @pl.when(g == 0)
    def _():
        _rms_rows(x_ref, ng_ref, h_s)

    u = jnp.dot(h_s[...], wu_ref[...], preferred_element_type=F32)
    first = (i % tiles_per_batch) == 0

    pooled_s[...] = jnp.dot(pmat_ref[0], u.astype(BF16), preferred_element_type=F32).astype(BF16)

    @pl.when(first)
    def _():
        edge_s[0:POOL_HEAD, :] = jnp.zeros((POOL_HEAD, COL_BLOCK), F32)

    @pl.when(jnp.logical_not(first))
    def _():
        edge_s[0:POOL_HEAD, :] = carry_s[g]

    edge_s[POOL_HEAD:, :] = u[0:POOL_HEAD, :]
    n_seen = ((i % tiles_per_batch) * tm + 1
              + lax.broadcasted_iota(jnp.int32, (POOL_HEAD, LANE_CHUNK), 0))
    for k, w in enumerate(POOL_WINDOWS):
        @pl.when(g == k)
        def _(w=w):
            for l0 in range(0, COL_BLOCK, LANE_CHUNK):
                pooled_s[0:POOL_HEAD, l0:l0 + LANE_CHUNK] = _pool_rows(
                    edge_s[:, l0:l0 + LANE_CHUNK], w, n_seen).astype(BF16)

    @pl.when((i % tiles_per_batch) == tiles_per_batch - 1)
    def _():
        col0 = pl.multiple_of(g * COL_BLOCK, COL_BLOCK)
        st_ref[i // tiles_per_batch, :, pl.ds(col0, COL_BLOCK)] = u[tm - POOL_STATE:tm, :]

    carry_s[g] = u[tm - POOL_HEAD:tm, :]
    _pool_finish(h_s, pooled_s[...], wg_ref, wgrp_ref, sc_ref, z_ref)


def _pool_in_sample_kernel(x_ref, ng_ref, wu_ref, wg_ref, wgrp_ref, sc_ref, st_in_ref,
                           z_ref, st_ref, h_s, ext_s, pooled_s, *, nb, tb):
    g = pl.program_id(1)

    @pl.when(g == 0)
    def _():
        _rms_rows(x_ref, ng_ref, h_s)

    u = jnp.dot(h_s[...], wu_ref[...], preferred_element_type=F32)

    ext_s[:, 0:1, :] = jnp.zeros((nb, 1, COL_BLOCK), F32)
    ext_s[:, 1:POOL_HEAD, :] = st_in_ref[...]
    for b in range(nb):
        ext_s[b, POOL_HEAD:, :] = u[b * tb:(b + 1) * tb, :]

    n_seen = POOL_STATE + 1 + lax.broadcasted_iota(jnp.int32, (tb, LANE_CHUNK), 0)
    for k, w in enumerate(POOL_WINDOWS):
        @pl.when(g == k)
        def _(w=w):
            for b in range(nb):
                for l0 in range(0, COL_BLOCK, LANE_CHUNK):
                    pooled_s[b * tb:(b + 1) * tb, l0:l0 + LANE_CHUNK] = _pool_rows(
                        ext_s[b, :, l0:l0 + LANE_CHUNK], w, n_seen)

    col0 = pl.multiple_of(g * COL_BLOCK, COL_BLOCK)
    for b in range(nb):
        st_ref[b, :, pl.ds(col0, COL_BLOCK)] = ext_s[b, tb + 1:tb + POOL_HEAD, :]
    _pool_finish(h_s, pooled_s[...].astype(BF16), wg_ref, wgrp_ref, sc_ref, z_ref)


def _pool_matrices(tm):
    t = jnp.arange(tm)[:, None]
    j = jnp.arange(tm)[None, :]
    mats = [jnp.where((t - j >= 0) & (t - j < w), 1.0 / w, 0.0) - (t == j) for w in POOL_WINDOWS]
    return jnp.stack(mats).astype(BF16)


def _pool_in(x2d, norm_g, w_in, w_grp, scale, state, *, n_batch, seq, tm):
    m = x2d.shape[0]
    n_groups = len(POOL_WINDOWS)
    in_specs = [
        pl.BlockSpec((tm, D_MODEL), lambda i, g: (i, 0)),
        pl.BlockSpec((1, D_MODEL), lambda i, g: (0, 0)),
        pl.BlockSpec((D_MODEL, COL_BLOCK), lambda i, g: (0, g)),
        pl.BlockSpec((D_MODEL, COL_BLOCK), lambda i, g: (0, n_groups + g)),
        pl.BlockSpec((1, POOL_GROUP, POOL_GROUP), lambda i, g: (g, 0, 0)),
        pl.BlockSpec((1, COL_BLOCK), lambda i, g: (0, g)),
    ]
    args = [x2d, norm_g.reshape(1, D_MODEL), w_in, w_in, w_grp, scale.reshape(1, D_INNER)]
    if state is None:
        assert seq % tm == 0 and tm >= POOL_HEAD
        in_specs.append(pl.BlockSpec((1, tm, tm), lambda i, g: (g, 0, 0)))
        args.append(_pool_matrices(tm))
        scratch = [
            pltpu.VMEM((tm, D_MODEL), BF16),
            pltpu.VMEM((2 * POOL_HEAD, COL_BLOCK), F32),
            pltpu.VMEM((tm, COL_BLOCK), BF16),
            pltpu.VMEM((n_groups, POOL_HEAD, COL_BLOCK), F32),
        ]
        kern = functools.partial(_pool_in_prompt_kernel, tm=tm, tiles_per_batch=seq // tm)
    else:
        assert tm == m
        in_specs.append(pl.BlockSpec((n_batch, POOL_STATE, COL_BLOCK), lambda i, g: (0, 0, g)))
        args.append(state)
        scratch = [
            pltpu.VMEM((tm, D_MODEL), BF16),
            pltpu.VMEM((n_batch, POOL_HEAD + seq, COL_BLOCK), F32),
            pltpu.VMEM((tm, COL_BLOCK), F32),
        ]
        kern = functools.partial(_pool_in_sample_kernel, nb=n_batch, tb=seq)
    return pl.pallas_call(
        kern,
        grid=(m // tm, n_groups),
        in_specs=in_specs,
        out_specs=[
            pl.BlockSpec((tm, COL_BLOCK), lambda i, g: (i, g)),
            pl.BlockSpec((n_batch, POOL_STATE, D_INNER), lambda i, g: (0, 0, 0)),
        ],
        out_shape=[
            jax.ShapeDtypeStruct((m, D_INNER), BF16),
            jax.ShapeDtypeStruct((n_batch, POOL_STATE, D_INNER), F32),
        ],
        scratch_shapes=scratch,
        compiler_params=_params(("arbitrary", "arbitrary")),
        name="pool_in",
    )(*args)


def _out_proj_kernel(z_ref, w_ref, x_ref, o_ref):
    o_ref[...] = x_ref[...] + jnp.dot(z_ref[...], w_ref[...], preferred_element_type=F32)


def _out_proj(z, w, x2d, *, tm, tn):
    m, k = z.shape
    n = w.shape[1]
    return pl.pallas_call(
        _out_proj_kernel,
        grid=(m // tm, n // tn),
        in_specs=[
            pl.BlockSpec((tm, k), lambda i, j: (i, 0)),
            pl.BlockSpec((k, tn), lambda i, j: (0, j)),
            pl.BlockSpec((tm, tn), lambda i, j: (i, j)),
        ],
        out_specs=pl.BlockSpec((tm, tn), lambda i, j: (i, j)),
        out_shape=jax.ShapeDtypeStruct((m, n), F32),
        compiler_params=_params(("arbitrary", "arbitrary")),
        name="out_proj",
    )(z, w, x2d)


def _ln_out_proj_kernel(y_ref, gate_ref, lg_ref, lb_ref, w_ref, x_ref, o_ref, z_s):
    @pl.when(pl.program_id(1) == 0)
    def _():
        def rows(it, carry):
            r0 = pl.multiple_of(it * BF16_ROWS, BF16_ROWS)
            y = y_ref[pl.ds(r0, BF16_ROWS), :]
            yc = y - jnp.mean(y, axis=-1, keepdims=True)
            var = jnp.mean(yc * yc, axis=-1, keepdims=True)
            ln = yc * lax.rsqrt(var + NORM_EPS) * lg_ref[...] + lb_ref[...]
            gate = gate_ref[pl.ds(r0, BF16_ROWS), :].astype(F32)
            z_s[pl.ds(r0, BF16_ROWS), :] = (_silu(ln) * _silu(gate)).astype(BF16)
            return carry

        lax.fori_loop(0, z_s.shape[0] // BF16_ROWS, rows, 0, unroll=2)

    o_ref[...] = x_ref[...] + jnp.dot(z_s[...], w_ref[...], preferred_element_type=F32)


def _ln_out_proj(y, gate, ln_g, ln_b, w, x2d, *, tm, tn):
    m, k = y.shape
    n = w.shape[1]
    return pl.pallas_call(
        _ln_out_proj_kernel,
        grid=(m // tm, n // tn),
        in_specs=[
            pl.BlockSpec((tm, k), lambda i, j: (i, 0)),
            pl.BlockSpec((tm, k), lambda i, j: (i, 0)),
            pl.BlockSpec((1, k), lambda i, j: (0, 0)),
            pl.BlockSpec((1, k), lambda i, j: (0, 0)),
            pl.BlockSpec((k, tn), lambda i, j: (0, j)),
            pl.BlockSpec((tm, tn), lambda i, j: (i, j)),
        ],
        out_specs=pl.BlockSpec((tm, tn), lambda i, j: (i, j)),
        out_shape=jax.ShapeDtypeStruct((m, n), F32),
        scratch_shapes=[pltpu.VMEM((tm, k), BF16)],
        compiler_params=_params(("arbitrary", "arbitrary")),
        name="ln_out_proj",
    )(y, gate, ln_g.reshape(1, k), ln_b.reshape(1, k), w, x2d)


def _head_norm(y, g128):
    lane = lax.broadcasted_iota(jnp.int32, (1, LANE), 1)
    low = lane < ATT_HEAD_DIM
    outs = []
    for c in range(y.shape[1] // LANE):
        blk = y[:, c * LANE:(c + 1) * LANE]
        sq = blk * blk
        s_lo = jnp.sum(jnp.where(low, sq, 0.0), axis=-1, keepdims=True)
        s_hi = jnp.sum(jnp.where(low, 0.0, sq), axis=-1, keepdims=True)
        ms = jnp.where(low, s_lo, s_hi) * (1.0 / ATT_HEAD_DIM)
        outs.append(blk * lax.rsqrt(ms + NORM_EPS) * g128)
    return jnp.concatenate(outs, axis=1)


def _attn_in_kernel(x_ref, ng_ref, w_ref, wkt_ref, qg_ref, kg_ref, q_ref, kt_ref, v_ref, vn_ref, gate_ref, h_s,
                    *, q_dtype, v_dtype, tm):
    j = pl.program_id(1)

    @pl.when(j == 0)
    def _():
        _rms_rows(x_ref, ng_ref, h_s)

    @pl.when(j < 2)
    def _():
        y = jnp.dot(h_s[...], w_ref[...], preferred_element_type=F32)
        q_ref[...] = (_head_norm(y, qg_ref[...]) * QK_SCALE).astype(q_dtype)

    @pl.when(jnp.logical_and(j >= 2, j < 4))
    def _():
        yt = lax.dot_general(wkt_ref[...], h_s[...], (((1,), (1,)), ((), ())), preferred_element_type=F32)
        y3 = yt.reshape(COL_BLOCK // ATT_HEAD_DIM, ATT_HEAD_DIM, tm)
        ms = jnp.mean(y3 * y3, axis=1, keepdims=True)
        kt_ref[...] = (y3 * lax.rsqrt(ms + NORM_EPS) * kg_ref[...][None]).reshape(COL_BLOCK, tm)

    for jj in (4, 5):
        @pl.when(j == jj)
        def _(jj=jj):
            y = jnp.dot(h_s[...], w_ref[...], preferred_element_type=F32)
            v_ref[...] = y.astype(v_dtype)
            if vn_ref is not None:
                for hl in range(COL_BLOCK // LANE):
                    hh = (jj - 4) * (COL_BLOCK // LANE) + hl
                    vn_ref[pl.ds(hh, tm, stride=ATT_HEADS), :] = y[:, hl * LANE:(hl + 1) * LANE]

    @pl.when(j >= 6)
    def _():
        gate_ref[...] = jnp.dot(h_s[...], w_ref[...], preferred_element_type=F32)


def _attn_in(x2d, norm_g, w_in, w_kt, q_norm_g, k_norm_g, *, n_batch, seq, tm, q_dtype, v_dtype, native_v):
    m = x2d.shape[0]
    tiles_per_batch = seq // tm
    n_col = w_in.shape[1] // COL_BLOCK

    def out_spec(first):
        return pl.BlockSpec((tm, COL_BLOCK), lambda i, j: (i, jnp.clip(j - first, 0, 1)))

    out_specs = [
        out_spec(0),
        pl.BlockSpec((None, COL_BLOCK, tm),
                     lambda i, j: (i // tiles_per_batch, jnp.clip(j - 2, 0, 1), i % tiles_per_batch)),
        out_spec(4),
    ]
    out_shape = [
        jax.ShapeDtypeStruct((m, D_MODEL), q_dtype),
        jax.ShapeDtypeStruct((n_batch, D_MODEL, seq), F32),
        jax.ShapeDtypeStruct((m, D_MODEL), v_dtype),
    ]
    if native_v:
        out_specs.append(pl.BlockSpec((tm * ATT_HEADS, LANE), lambda i, j: (i, 0)))
        out_shape.append(jax.ShapeDtypeStruct((m * ATT_HEADS, LANE), F32))
    out_specs.append(out_spec(6))
    out_shape.append(jax.ShapeDtypeStruct((m, D_MODEL), F32))

    def kern(*refs):
        ins, outs, scratch = refs[:6], refs[6:-1], refs[-1]
        if native_v:
            q_ref, kt_ref, v_ref, vn_ref, gate_ref = outs
        else:
            (q_ref, kt_ref, v_ref, gate_ref), vn_ref = outs, None
        _attn_in_kernel(*ins, q_ref, kt_ref, v_ref, vn_ref, gate_ref, scratch,
                        q_dtype=q_dtype, v_dtype=v_dtype, tm=tm)

    return pl.pallas_call(
        kern,
        grid=(m // tm, n_col),
        in_specs=[
            pl.BlockSpec((tm, D_MODEL), lambda i, j: (i, 0)),
            pl.BlockSpec((1, D_MODEL), lambda i, j: (0, 0)),
            pl.BlockSpec((D_MODEL, COL_BLOCK), lambda i, j: (0, jnp.where(jnp.logical_and(j >= 2, j < 4), 1, j))),
            pl.BlockSpec((COL_BLOCK, D_MODEL), lambda i, j: (jnp.clip(j - 2, 0, 1), 0)),
            pl.BlockSpec((1, LANE), lambda i, j: (0, 0)),
            pl.BlockSpec((ATT_HEAD_DIM, 1), lambda i, j: (0, 0)),
        ],
        out_specs=out_specs,
        out_shape=out_shape,
        scratch_shapes=[pltpu.VMEM((tm, D_MODEL), BF16)],
        compiler_params=_params(("arbitrary", "arbitrary")),
        name="attn_in",
    )(x2d, norm_g.reshape(1, D_MODEL), w_in, w_kt, jnp.tile(q_norm_g, 2).reshape(1, LANE),
      k_norm_g.reshape(ATT_HEAD_DIM, 1))


def _lambda(lq1_ref, lk1_ref, lq2_ref, lk2_ref):
    a = jnp.sum(lq1_ref[...] * lk1_ref[...], axis=-1, keepdims=True)
    b = jnp.sum(lq2_ref[...] * lk2_ref[...], axis=-1, keepdims=True)
    return jnp.exp(a) - jnp.exp(b) + LAMBDA_INIT


def _sub_ln_gate(o, subln_g, gate):
    ms = jnp.mean(o * o, axis=-1, keepdims=True)
    o = o * lax.rsqrt(ms + NORM_EPS) * subln_g * (1.0 - LAMBDA_INIT)
    return o * _silu(gate)


def _attn_prompt_kernel(q_ref, kt_ref, v_ref, gate_ref, sg_ref, lq1_ref, lk1_ref, lq2_ref, lk2_ref,
                        z_ref, ka_s, m_s, l_s, acc_s, *, tq, tk):
    h = pl.program_id(1)
    qi = pl.program_id(2)
    rep = tk // LANE
    seq = kt_ref.shape[1]
    slope2 = jnp.exp2(-0.5 * jnp.full((1, LANE), h + 1, jnp.int32).astype(F32)) * LOG2E

    @pl.when(qi == 0)
    def _():
        col = lax.broadcasted_iota(jnp.int32, (BF16_ROWS, seq), 1) % tk
        row = lax.broadcasted_iota(jnp.int32, (BF16_ROWS, seq), 0)
        b = pltpu.repeat(slope2, seq // LANE, axis=1) * col.astype(F32)
        hi = b.astype(BF16).astype(F32)
        mid = (b - hi).astype(BF16).astype(F32)
        lo = b - hi - mid
        rows = jnp.where(row == 0, hi, jnp.where(row == 1, mid, jnp.where(row == 2, lo, 0.0))).astype(BF16)
        zeros = jnp.zeros((ATT_HEAD_DIM - BF16_ROWS, seq), BF16)
        ka_s[0, 0:ATT_HEAD_DIM, :] = kt_ref[0:ATT_HEAD_DIM, :].astype(BF16)
        ka_s[0, ATT_HEAD_DIM:ATT_HEAD_DIM + BF16_ROWS, :] = rows
        ka_s[0, ATT_HEAD_DIM + BF16_ROWS:, :] = zeros
        ka_s[1, 0:BF16_ROWS, :] = rows
        ka_s[1, BF16_ROWS:ATT_HEAD_DIM, :] = zeros
        ka_s[1, ATT_HEAD_DIM:, :] = kt_ref[ATT_HEAD_DIM:, :].astype(BF16)

    q = q_ref[...]
    lane = lax.broadcasted_iota(jnp.int32, (1, LANE), 1)
    low = lane < ATT_HEAD_DIM
    zero = jnp.zeros_like(q)
    ones1 = jnp.where(jnp.logical_and(lane >= ATT_HEAD_DIM, lane < ATT_HEAD_DIM + 3), 1.0, 0.0).astype(BF16)
    ones2 = jnp.where(lane < 3, 1.0, 0.0).astype(BF16)
    qa = (jnp.where(low, q, zero) + ones1, jnp.where(low, zero, q) + ones2)

    m_s[...] = jnp.full(m_s.shape, NEG_INF, F32)
    l_s[...] = jnp.zeros(l_s.shape, F32)
    acc_s[...] = jnp.zeros(acc_s.shape, F32)

    def tile(kj, masked):
        start = pl.multiple_of(kj * tk, tk)
        vt = v_ref[pl.ds(start, tk), :]
        shift = slope2 * jnp.full((1, LANE), kj * tk - qi * tq, jnp.int32).astype(F32)
        for c in range(2):
            s = jnp.dot(qa[c], ka_s[c, :, pl.ds(start, tk)], preferred_element_type=F32)
            if masked:
                row_i = lax.broadcasted_iota(jnp.int32, (tq, tk), 0)
                col_j = lax.broadcasted_iota(jnp.int32, (tq, tk), 1) + (kj * tk - qi * tq)
                s = jnp.where(col_j <= row_i, s, NEG_INF)
            m_old = m_s[c]
            m_new = jnp.maximum(m_old, jnp.max(s, axis=-1, keepdims=True) + shift)
            alpha = jnp.exp2(m_old - m_new)
            p = jnp.exp2(s - pltpu.repeat(m_new - shift, rep, axis=1))
            l_s[c] = alpha * l_s[c] + jnp.sum(p, axis=-1, keepdims=True)
            acc_s[c] = alpha * acc_s[c] + jnp.dot(p.astype(BF16), vt, preferred_element_type=F32)
            m_s[c] = m_new

    n_full = (qi * tq) // tk
    n_diag = tq // tk

    def body(kj, carry):
        tile(kj, False)
        return carry

    lax.fori_loop(0, n_full, body, 0)
    for d in range(n_diag):
        tile(n_full + d, True)

    lam = _lambda(lq1_ref, lk1_ref, lq2_ref, lk2_ref)
    o = acc_s[0] / l_s[0] - lam * (acc_s[1] / l_s[1])
    z_ref[...] = _sub_ln_gate(o, sg_ref[...], gate_ref[...]).astype(BF16)


def _attn_prompt(q, kt, v, gate, subln_g, lams, *, n_batch, seq, tq, tk):
    assert tq % tk == 0 and seq % tq == 0
    vec = pl.BlockSpec((1, ATT_HEAD_DIM), lambda b, h, i: (0, 0))
    kern = functools.partial(_attn_prompt_kernel, tq=tq, tk=tk)
    return pl.pallas_call(
        kern,
        grid=(n_batch, ATT_HEADS, seq // tq),
        in_specs=[
            pl.BlockSpec((None, tq, LANE), lambda b, h, i: (b, i, h)),
            pl.BlockSpec((None, LANE, seq), lambda b, h, i: (b, h, 0)),
            pl.BlockSpec((None, seq, LANE), lambda b, h, i: (b, 0, h)),
            pl.BlockSpec((None, tq, LANE), lambda b, h, i: (b, i, h)),
            pl.BlockSpec((1, LANE), lambda b, h, i: (0, 0)),
            vec, vec, vec, vec,
        ],
        out_specs=pl.BlockSpec((None, tq, LANE), lambda b, h, i: (b, i, h)),
        out_shape=jax.ShapeDtypeStruct((n_batch, seq, D_MODEL), BF16),
        scratch_shapes=[
            pltpu.VMEM((2, LANE, seq), BF16),
            pltpu.VMEM((2, tq, LANE), F32),
            pltpu.VMEM((2, tq, LANE), F32),
            pltpu.VMEM((2, tq, LANE), F32),
        ],
        compiler_params=_params(("arbitrary", "arbitrary", "arbitrary")),
        name="attn_prompt",
    )(q, kt, v, gate, subln_g.reshape(1, LANE), *[t.reshape(1, ATT_HEAD_DIM) for t in lams])


def _attn_sample_kernel(pt_ref, q_ref, knt_ref, vn_ref, gate_ref, sg_ref,
                        lq1_ref, lk1_ref, lq2_ref, lk2_ref, *rest, pages, n_steps, past_len, t_new):
    k_refs = rest[:pages]
    v_refs = rest[pages:2 * pages]
    z_ref, q3_s, vh_s, slope_s, m_s, l_s, acc_s = rest[2 * pages:]
    b = pl.program_id(0)
    p = pl.program_id(1)
    n_rows = 2 * t_new
    shape3 = (ATT_HEADS, n_rows, LANE)

    @pl.when(p == 0)
    def _():
        row_c = lax.broadcasted_iota(jnp.int32, (n_rows, LANE), 0) // t_new
        lane_c = lax.broadcasted_iota(jnp.int32, (n_rows, LANE), 1) // ATT_HEAD_DIM
        for hh in range(ATT_HEADS):
            blk = q_ref[:, hh * LANE:(hh + 1) * LANE]
            both = jnp.concatenate([blk, blk], axis=0)
            q3_s[hh] = jnp.where(row_c == lane_c, both, 0.0).astype(BF16)
        head = lax.broadcasted_iota(jnp.int32, shape3, 0)
        slope_s[...] = jnp.exp2(-0.5 * (head + 1).astype(F32)) * LOG2E
        m_s[...] = jnp.full(shape3, NEG_INF, F32)
        l_s[...] = jnp.zeros(shape3, F32)
        acc_s[...] = jnp.zeros(shape3, F32)

    q3 = q3_s[...]

    def update(s, vb):
        m_old = m_s[...]
        m_new = jnp.maximum(m_old, jnp.max(s, axis=-1, keepdims=True))
        alpha = jnp.exp2(m_old - m_new)
        t = s.shape[-1]
        m_all = pltpu.repeat(m_new, t // LANE, axis=2) if t % LANE == 0 else m_new[:, :, 0:t]
        pr = jnp.exp2(s - m_all)
        l_s[...] = alpha * l_s[...] + jnp.sum(pr, axis=-1, keepdims=True)
        pv = jnp.einsum("hrt,hte->hre", pr.astype(BF16), vb, preferred_element_type=F32)
        acc_s[...] = alpha * acc_s[...] + pv
        m_s[...] = m_new

    tok = lax.broadcasted_iota(jnp.int32, (1, 1, PAGE_SIZE), 2)
    slope = slope_s[...]
    scores = []
    for r in range(pages):
        k3 = k_refs[r][...].reshape(ATT_HEADS, LANE, PAGE_SIZE).astype(BF16)
        k_pos = ((p * pages + r) * PAGE_SIZE - past_len + tok).astype(F32)
        scores.append(jnp.einsum("hrk,hkt->hrt", q3, k3, preferred_element_type=F32) + slope * k_pos)
        for hh in range(ATT_HEADS):
            vh_s[hh, r * PAGE_SIZE:(r + 1) * PAGE_SIZE, :] = (
                v_refs[r][pl.ds(hh, PAGE_SIZE, stride=ATT_HEADS), :].astype(BF16))
    update(jnp.concatenate(scores, axis=-1), vh_s[...])

    @pl.when(p == n_steps - 1)
    def _():
        n_tok = knt_ref.shape[1]
        k3 = knt_ref[...].reshape(ATT_HEADS, LANE, n_tok).astype(BF16)
        tok_n = lax.broadcasted_iota(jnp.int32, (1, n_rows, n_tok), 2)
        row_t = lax.broadcasted_iota(jnp.int32, (1, n_rows, n_tok), 1) % t_new
        valid = jnp.logical_and(tok_n // t_new == b, tok_n % t_new <= row_t)
        s = jnp.einsum("hrk,hkt->hrt", q3, k3, preferred_element_type=F32)
        s = s + slope[:, :, 0:n_tok] * (tok_n % t_new).astype(F32)
        s = jnp.where(valid, s, NEG_INF)
        vb = jnp.stack([vn_ref[:, hh * LANE:(hh + 1) * LANE] for hh in range(ATT_HEADS)], axis=0).astype(BF16)
        update(s, vb)

        lam = _lambda(lq1_ref, lk1_ref, lq2_ref, lk2_ref)
        acc = acc_s[...]
        l = l_s[...]
        o = acc[:, 0:t_new, :] / l[:, 0:t_new, :] - lam * (acc[:, t_new:, :] / l[:, t_new:, :])
        for hh in range(ATT_HEADS):
            gate = gate_ref[:, hh * LANE:(hh + 1) * LANE]
            z_ref[:, hh * LANE:(hh + 1) * LANE] = _sub_ln_gate(o[hh], sg_ref[...], gate)


def _attn_sample(q, knt, v_new, gate, cache_kt, cache_v2, page_table, subln_g, lams, *, pages):
    n_batch, t_new, _ = q.shape
    n_pages = page_table.shape[1]
    n_steps = n_pages // pages
    n_rows = 2 * t_new
    n_tok = n_batch * t_new

    tok_spec = pl.BlockSpec((None, t_new, D_MODEL), lambda b, p, pt: (b, 0, 0))
    vec = pl.BlockSpec((1, ATT_HEAD_DIM), lambda b, p, pt: (0, 0))

    def page_spec(r):
        return pl.BlockSpec((None, D_MODEL, PAGE_SIZE), lambda b, p, pt: (pt[b, p * pages + r], 0, 0))

    kern = functools.partial(_attn_sample_kernel, pages=pages, n_steps=n_steps,
                             past_len=n_pages * PAGE_SIZE, t_new=t_new)
    grid_spec = pltpu.PrefetchScalarGridSpec(
        num_scalar_prefetch=1,
        grid=(n_batch, n_steps),
        in_specs=[tok_spec,
                  pl.BlockSpec((D_MODEL, n_tok), lambda b, p, pt: (0, 0)),
                  pl.BlockSpec((n_tok, D_MODEL), lambda b, p, pt: (0, 0)),
                  tok_spec,
                  pl.BlockSpec((1, LANE), lambda b, p, pt: (0, 0)), vec, vec, vec, vec]
        + [page_spec(r) for r in range(pages)] + [page_spec(r) for r in range(pages)],
        out_specs=tok_spec,
        scratch_shapes=[
            pltpu.VMEM((ATT_HEADS, n_rows, LANE), BF16),
            pltpu.VMEM((ATT_HEADS, pages * PAGE_SIZE, LANE), BF16),
            pltpu.VMEM((ATT_HEADS, n_rows, LANE), F32),
            pltpu.VMEM((ATT_HEADS, n_rows, LANE), F32),
            pltpu.VMEM((ATT_HEADS, n_rows, LANE), F32),
            pltpu.VMEM((ATT_HEADS, n_rows, LANE), F32),
        ],
    )
    return pl.pallas_call(
        kern,
        grid_spec=grid_spec,
        out_shape=jax.ShapeDtypeStruct((n_batch, t_new, D_MODEL), F32),
        compiler_params=_params(("arbitrary", "arbitrary")),
        name="attn_sample",
    )(page_table, q, knt, v_new, gate, subln_g.reshape(1, LANE),
      *[t.reshape(1, ATT_HEAD_DIM) for t in lams], *([cache_kt] * pages), *([cache_v2] * pages))


def _conv_in_kernel(*refs, nb, tb, tiles_per_batch, has_state):
    if has_state:
        (x_ref, ng_ref, wa_ref, wl_ref, wg_ref, cw_ref, cb_ref, st_in_ref,
         y_ref, gate_ref, st_ref, h_s, ext_s, wb_s) = refs
        carry_s = None
    else:
        (x_ref, ng_ref, wa_ref, wl_ref, wg_ref, cw_ref, cb_ref,
         y_ref, gate_ref, st_ref, h_s, ext_s, wb_s, carry_s) = refs
    i = pl.program_id(0)
    cb = pl.program_id(1)

    @pl.when(cb == 0)
    def _():
        _rms_rows(x_ref, ng_ref, h_s)

    a = jnp.dot(h_s[...], wa_ref[...], preferred_element_type=F32)
    glu = jnp.dot(h_s[...], wl_ref[...], preferred_element_type=F32)
    c = a * jax.nn.sigmoid(glu)

    head0 = CONV_HEAD - CONV_STATE
    if has_state:
        ext_s[:, 0:head0, :] = jnp.zeros((nb, head0, COL_BLOCK), F32)
        ext_s[:, head0:CONV_HEAD, :] = st_in_ref[...]
    else:
        first = (i % tiles_per_batch) == 0

        @pl.when(first)
        def _():
            ext_s[0, 0:CONV_HEAD, :] = jnp.zeros((CONV_HEAD, COL_BLOCK), F32)

        @pl.when(jnp.logical_not(first))
        def _():
            ext_s[0, 0:CONV_HEAD, :] = carry_s[cb]

    for b in range(nb):
        ext_s[b, CONV_HEAD:CONV_HEAD + tb, :] = c[b * tb:(b + 1) * tb, :]

    rc = min(tb, ROW_CHUNK)
    n_lane = COL_BLOCK // LANE_CHUNK
    n_row = tb // rc

    for j in range(CONV_WIDTH):
        wb_s[j] = jnp.broadcast_to(cw_ref[j:j + 1, :], (SUBLANES, COL_BLOCK))

    def chunk(it, carry):
        b = it // (n_row * n_lane)
        r0 = pl.multiple_of(((it // n_lane) % n_row) * rc, rc)
        l0 = pl.multiple_of((it % n_lane) * LANE_CHUNK, LANE_CHUNK)
        blk = ext_s[b, pl.ds(r0, rc + CONV_HEAD), pl.ds(l0, LANE_CHUNK)]
        acc = jnp.broadcast_to(cb_ref[:, pl.ds(l0, LANE_CHUNK)], (rc, LANE_CHUNK))
        for r in range(SUBLANES):
            shifted = blk if r == 0 else blk[r:r + rc + CONV_HEAD - SUBLANES, :]
            for j in range(CONV_WIDTH):
                if (head0 + j) % SUBLANES == r:
                    a = head0 + j - r
                    w_j = pltpu.repeat(wb_s[j, :, pl.ds(l0, LANE_CHUNK)], rc // SUBLANES, axis=0)
                    acc = acc + w_j * shifted[a:a + rc, :]
        y_ref[pl.ds(b * tb + r0, rc), pl.ds(l0, LANE_CHUNK)] = acc
        return carry

    lax.fori_loop(0, nb * n_row * n_lane, chunk, 0, unroll=2)
    col0 = pl.multiple_of(cb * COL_BLOCK, COL_BLOCK)
    if has_state:
        for b in range(nb):
            st_ref[b, :, pl.ds(col0, COL_BLOCK)] = ext_s[b, tb + head0:tb + CONV_HEAD, :]
    else:
        @pl.when((i % tiles_per_batch) == tiles_per_batch - 1)
        def _():
            st_ref[i // tiles_per_batch, :, pl.ds(col0, COL_BLOCK)] = ext_s[0, tb + head0:tb + CONV_HEAD, :]

        carry_s[cb] = ext_s[0, tb:tb + CONV_HEAD, :]

    gate_ref[...] = jnp.dot(h_s[...], wg_ref[...], preferred_element_type=F32).astype(BF16)


def _conv_in(x2d, norm_g, w_in, conv_w, conv_b, state, *, n_batch, seq, tm):
    m = x2d.shape[0]
    has_state = state is not None
    if has_state:
        assert tm == m
        nb, tb, tiles_per_batch = n_batch, seq, 1
    else:
        assert seq % tm == 0 and tm >= CONV_HEAD
        nb, tb, tiles_per_batch = 1, tm, seq // tm
    n_cb = D_INNER // COL_BLOCK
    in_specs = [
        pl.BlockSpec((tm, D_MODEL), lambda i, c: (i, 0)),
        pl.BlockSpec((1, D_MODEL), lambda i, c: (0, 0)),
        pl.BlockSpec((D_MODEL, COL_BLOCK), lambda i, c: (0, c)),
        pl.BlockSpec((D_MODEL, COL_BLOCK), lambda i, c: (0, n_cb + c)),
        pl.BlockSpec((D_MODEL, COL_BLOCK), lambda i, c: (0, 2 * n_cb + c)),
        pl.BlockSpec((CONV_WIDTH, COL_BLOCK), lambda i, c: (0, c)),
        pl.BlockSpec((1, COL_BLOCK), lambda i, c: (0, c)),
    ]
    args = [x2d, norm_g.reshape(1, D_MODEL), w_in, w_in, w_in, conv_w, conv_b.reshape(1, D_INNER)]
    scratch = [
        pltpu.VMEM((tm, D_MODEL), BF16),
        pltpu.VMEM((nb, CONV_HEAD + tb, COL_BLOCK), F32),
        pltpu.VMEM((CONV_WIDTH, SUBLANES, COL_BLOCK), F32),
    ]
    if has_state:
        in_specs.append(pl.BlockSpec((nb, CONV_STATE, COL_BLOCK), lambda i, c: (0, 0, c)))
        args.append(state)
    else:
        scratch.append(pltpu.VMEM((n_cb, CONV_HEAD, COL_BLOCK), F32))
    kern = functools.partial(_conv_in_kernel, nb=nb, tb=tb, tiles_per_batch=tiles_per_batch,
                             has_state=has_state)
    return pl.pallas_call(
        kern,
        grid=(m // tm, n_cb),
        in_specs=in_specs,
        out_specs=[
            pl.BlockSpec((tm, COL_BLOCK), lambda i, c: (i, c)),
            pl.BlockSpec((tm, COL_BLOCK), lambda i, c: (i, c)),
            pl.BlockSpec((n_batch, CONV_STATE, D_INNER), lambda i, c: (0, 0, 0)),
        ],
        out_shape=[
            jax.ShapeDtypeStruct((m, D_INNER), F32),
            jax.ShapeDtypeStruct((m, D_INNER), BF16),
            jax.ShapeDtypeStruct((n_batch, CONV_STATE, D_INNER), F32),
        ],
        scratch_shapes=scratch,
        compiler_params=_params(("arbitrary", "arbitrary")),
        name="conv_in",
    )(*args)


PROMPT_TM = 512
LN_TM = 512
LN_TN = 512
ATTN_TQ = 1024
ATTN_TK = 1024
SAMPLE_PAGES_PER_STEP = 8


def kernel(x_prompt, x_sample, state_pool_l0, cache_k_l1, cache_v_l1, state_conv_l2, state_pool_l3, page_table, norm_g_l0, w_in_l0, w_grp_l0, pool_scale_l0, w_out_l0, norm_g_l1, w_in_l1, q_norm_g_l1, k_norm_g_l1, lambda_q1_l1, lambda_k1_l1, lambda_q2_l1, lambda_k2_l1, subln_g_l1, w_out_l1, norm_g_l2, w_in_l2, conv_w_l2, conv_b_l2, ln_g_l2, ln_b_l2, w_out_l2, norm_g_l3, w_in_l3, w_grp_l3, pool_scale_l3, w_out_l3):
    nbp, seq, _ = x_prompt.shape
    nbs, t_new, _ = x_sample.shape
    mp, ms = nbp * seq, nbs * t_new
    xp = x_prompt.reshape(mp, D_MODEL)
    xs = x_sample.reshape(ms, D_MODEL)
    lams = (lambda_q1_l1, lambda_k1_l1, lambda_q2_l1, lambda_k2_l1)

    def bf(w):
        return w.astype(BF16)

    def pool_layer(xp, xs, state, norm_g, w_in, w_grp, scale, w_out):
        w_in, w_grp, w_out = bf(w_in), bf(w_grp), bf(w_out)
        zp, stp = _pool_in(xp, norm_g, w_in, w_grp, scale, None, n_batch=nbp, seq=seq, tm=PROMPT_TM)
        zs, sts = _pool_in(xs, norm_g, w_in, w_grp, scale, state, n_batch=nbs, seq=t_new, tm=ms)
        xp = _out_proj(zp, w_out, xp, tm=PROMPT_TM, tn=COL_BLOCK)
        xs = _out_proj(zs, w_out, xs, tm=ms, tn=COL_BLOCK)
        return xp, xs, stp, sts

    xp, xs, pool0_p, pool0_s = pool_layer(xp, xs, state_pool_l0, norm_g_l0, w_in_l0, w_grp_l0,
                                          pool_scale_l0, w_out_l0)

    w_in, w_out = bf(w_in_l1), bf(w_out_l1)
    w_kt = bf(lax.slice(w_in_l1, (0, D_MODEL), (D_MODEL, 2 * D_MODEL))).T
    qp, ktp, vp, vnp, gp = _attn_in(xp, norm_g_l1, w_in, w_kt, q_norm_g_l1, k_norm_g_l1, n_batch=nbp, seq=seq,
                                    tm=PROMPT_TM, q_dtype=BF16, v_dtype=BF16, native_v=True)
    qs, kts, vs, gs = _attn_in(xs, norm_g_l1, w_in, w_kt, q_norm_g_l1, k_norm_g_l1, n_batch=1, seq=ms,
                               tm=ms, q_dtype=F32, v_dtype=F32, native_v=False)
    shp = (nbp, seq, D_MODEL)
    zp = _attn_prompt(qp.reshape(shp), ktp, vp.reshape(shp), gp.reshape(shp), subln_g_l1, lams,
                      n_batch=nbp, seq=seq, tq=ATTN_TQ, tk=ATTN_TK)
    n_phys = cache_k_l1.shape[0]
    cache_kt = jnp.transpose(cache_k_l1, (0, 2, 3, 4, 1)).reshape(n_phys, D_MODEL, PAGE_SIZE)
    cache_v2 = cache_v_l1.reshape(n_phys, PAGE_SIZE * ATT_HEADS, ATT_VDIM)
    shs = (nbs, t_new, D_MODEL)
    zs = _attn_sample(qs.reshape(shs), kts[0], vs, gs.reshape(shs), cache_kt, cache_v2, page_table,
                      subln_g_l1, lams, pages=SAMPLE_PAGES_PER_STEP)
    xp = _out_proj(zp.reshape(mp, D_MODEL), w_out, xp, tm=PROMPT_TM, tn=COL_BLOCK)
    xs = _out_proj(zs.reshape(ms, D_MODEL).astype(BF16), w_out, xs, tm=ms, tn=COL_BLOCK)
    new_k_p = jnp.transpose(ktp.reshape(nbp, ATT_HEADS, 2, ATT_HEAD_DIM, seq), (0, 4, 1, 2, 3))
    new_v_p = vnp.reshape(nbp, seq, ATT_HEADS, ATT_VDIM)
    new_k_s = kts[0].T.reshape(nbs, t_new, ATT_HEADS, 2, ATT_HEAD_DIM)
    new_v_s = vs.reshape(nbs, t_new, ATT_HEADS, ATT_VDIM)

    w_in, w_out = bf(w_in_l2), bf(w_out_l2)
    yp, gp, conv_p = _conv_in(xp, norm_g_l2, w_in, conv_w_l2, conv_b_l2, None, n_batch=nbp, seq=seq,
                              tm=PROMPT_TM)
    ys, gs, conv_s = _conv_in(xs, norm_g_l2, w_in, conv_w_l2, conv_b_l2, state_conv_l2, n_batch=nbs,
                              seq=t_new, tm=ms)
    xp = _ln_out_proj(yp, gp, ln_g_l2, ln_b_l2, w_out, xp, tm=LN_TM, tn=LN_TN)
    xs = _ln_out_proj(ys, gs, ln_g_l2, ln_b_l2, w_out, xs, tm=ms, tn=LN_TN)

    xp, xs, pool3_p, pool3_s = pool_layer(xp, xs, state_pool_l3, norm_g_l3, w_in_l3, w_grp_l3,
                                          pool_scale_l3, w_out_l3)

    return (xp.reshape(nbp, seq, D_MODEL), xs.reshape(nbs, t_new, D_MODEL), pool0_p, pool0_s,
            new_k_p, new_v_p, new_k_s, new_v_s, conv_p, conv_s, pool3_p, pool3_s)
```

```python
import functools
import math

import jax
import jax.numpy as jnp
from jax import lax
from jax.experimental import pallas as pl
from jax.experimental.pallas import tpu as pltpu

F32 = jnp.float32
BF16 = jnp.bfloat16

D_MODEL = 2048
D_INNER = 4096
POOL_WINDOWS = (2, 4, 8, 16)
POOL_GROUP = 1024
POOL_STATE = 15
ATT_HEADS = 16
ATT_HEAD_DIM = 64
ATT_VDIM = 128
LAMBDA_INIT = 0.8 - 0.6 * math.exp(-0.3 * 1)
LOG2E = math.log2(math.e)
QK_SCALE = ATT_HEAD_DIM ** -0.5 * LOG2E
CONV_WIDTH = 31
CONV_STATE = 30
NORM_EPS = 1e-6
NEG_INF = -1e30
PAGE_SIZE = 128

VMEM_LIMIT_BYTES = 58 * 1024 * 1024
LANE = 128
SUBLANES = 8
BF16_ROWS = 16
COL_BLOCK = 1024
POOL_HEAD = 16
CONV_HEAD = 32
ROW_CHUNK = 32
LANE_CHUNK = 256


def _params(sem):
    return pltpu.CompilerParams(dimension_semantics=sem, vmem_limit_bytes=VMEM_LIMIT_BYTES)


def _rms_rows(x_ref, g_ref, h_ref):
    xf = x_ref[...]
    ms = jnp.mean(xf * xf, axis=-1, keepdims=True)
    h_ref[...] = (xf * lax.rsqrt(ms + NORM_EPS) * g_ref[...]).astype(BF16)


def _silu(x):
    return x * jax.nn.sigmoid(x)


def _pool_rows(blk, w, n_seen):
    rows = blk.shape[0] - POOL_HEAD
    cur = blk[POOL_HEAD:, :]
    acc = cur
    for j in range(1, w):
        acc = acc + blk[POOL_HEAD - j:POOL_HEAD - j + rows, :]
    cnt = jnp.minimum(n_seen, w).astype(F32)
    return jnp.where(n_seen >= w, acc * (1.0 / w), acc / cnt) - cur


def _pool_finish(h_s, pooled, wg_ref, wgrp_ref, sc_ref, z_ref):
    mixed = jnp.dot(pooled, wgrp_ref[0], preferred_element_type=F32) * sc_ref[...]
    gate = jnp.dot(h_s[...], wg_ref[...], preferred_element_type=F32)
    z_ref[...] = (mixed * _silu(gate)).astype(BF16)


def _pool_in_prompt_kernel(x_ref, ng_ref, wu_ref, wg_ref, wgrp_ref, sc_ref, pmat_ref,
                           z_ref, st_ref, h_s, edge_s, pooled_s, carry_s, *, tm, tiles_per_batch):
    i = pl.program_id(0)
    g = pl.program_id(1)

    @pl.when(g == 0)
    def _():
        _rms_rows(x_ref, ng_ref, h_s)

    u = jnp.dot(h_s[...], wu_ref[...], preferred_element_type=F32)
    first = (i % tiles_per_batch) == 0

    pooled_s[...] = jnp.dot(pmat_ref[0], u.astype(BF16), preferred_element_type=F32).astype(BF16)

    @pl.when(first)
    def _():
        edge_s[0:POOL_HEAD, :] = jnp.zeros((POOL_HEAD, COL_BLOCK), F32)

    @pl.when(jnp.logical_not(first))
    def _():
        edge_s[0:POOL_HEAD, :] = carry_s[g]

    edge_s[POOL_HEAD:, :] = u[0:POOL_HEAD, :]
    n_seen = ((i % tiles_per_batch) * tm + 1
              + lax.broadcasted_iota(jnp.int32, (POOL_HEAD, LANE_CHUNK), 0))
    for k, w in enumerate(POOL_WINDOWS):
        @pl.when(g == k)
        def _(w=w):
            for l0 in range(0, COL_BLOCK, LANE_CHUNK):
                pooled_s[0:POOL_HEAD, l0:l0 + LANE_CHUNK] = _pool_rows(
                    edge_s[:, l0:l0 + LANE_CHUNK], w, n_seen).astype(BF16)

    @pl.when((i % tiles_per_batch) == tiles_per_batch - 1)
    def _():
        col0 = pl.multiple_of(g * COL_BLOCK, COL_BLOCK)
        st_ref[i // tiles_per_batch, :, pl.ds(col0, COL_BLOCK)] = u[tm - POOL_STATE:tm, :]

    carry_s[g] = u[tm - POOL_HEAD:tm, :]
    _pool_finish(h_s, pooled_s[...], wg_ref, wgrp_ref, sc_ref, z_ref)


def _pool_in_sample_kernel(x_ref, ng_ref, wu_ref, wg_ref, wgrp_ref, sc_ref, st_in_ref,
                           z_ref, st_ref, h_s, ext_s, pooled_s, *, nb, tb):
    g = pl.program_id(1)

    @pl.when(g == 0)
    def _():
        _rms_rows(x_ref, ng_ref, h_s)

    u = jnp.dot(h_s[...], wu_ref[...], preferred_element_type=F32)

    ext_s[:, 0:1, :] = jnp.zeros((nb, 1, COL_BLOCK), F32)
    ext_s[:, 1:POOL_HEAD, :] = st_in_ref[...]
    for b in range(nb):
        ext_s[b, POOL_HEAD:, :] = u[b * tb:(b + 1) * tb, :]

    n_seen = POOL_STATE + 1 + lax.broadcasted_iota(jnp.int32, (tb, LANE_CHUNK), 0)
    for k, w in enumerate(POOL_WINDOWS):
        @pl.when(g == k)
        def _(w=w):
            for b in range(nb):
                for l0 in range(0, COL_BLOCK, LANE_CHUNK):
                    pooled_s[b * tb:(b + 1) * tb, l0:l0 + LANE_CHUNK] = _pool_rows(
                        ext_s[b, :, l0:l0 + LANE_CHUNK], w, n_seen)

    col0 = pl.multiple_of(g * COL_BLOCK, COL_BLOCK)
    for b in range(nb):
        st_ref[b, :, pl.ds(col0, COL_BLOCK)] = ext_s[b, tb + 1:tb + POOL_HEAD, :]
    _pool_finish(h_s, pooled_s[...].astype(BF16), wg_ref, wgrp_ref, sc_ref, z_ref)


def _pool_matrices(tm):
    t = jnp.arange(tm)[:, None]
    j = jnp.arange(tm)[None, :]
    mats = [jnp.where((t - j >= 0) & (t - j < w), 1.0 / w, 0.0) - (t == j) for w in POOL_WINDOWS]
    return jnp.stack(mats).astype(BF16)


def _pool_in(x2d, norm_g, w_in, w_grp, scale, state, *, n_batch, seq, tm):
    m = x2d.shape[0]
    n_groups = len(POOL_WINDOWS)
    in_specs = [
        pl.BlockSpec((tm, D_MODEL), lambda i, g: (i, 0)),
        pl.BlockSpec((1, D_MODEL), lambda i, g: (0, 0)),
        pl.BlockSpec((D_MODEL, COL_BLOCK), lambda i, g: (0, g)),
        pl.BlockSpec((D_MODEL, COL_BLOCK), lambda i, g: (0, n_groups + g)),
        pl.BlockSpec((1, POOL_GROUP, POOL_GROUP), lambda i, g: (g, 0, 0)),
        pl.BlockSpec((1, COL_BLOCK), lambda i, g: (0, g)),
    ]
    args = [x2d, norm_g.reshape(1, D_MODEL), w_in, w_in, w_grp, scale.reshape(1, D_INNER)]
    if state is None:
        assert seq % tm == 0 and tm >= POOL_HEAD
        in_specs.append(pl.BlockSpec((1, tm, tm), lambda i, g: (g, 0, 0)))
        args.append(_pool_matrices(tm))
        scratch = [
            pltpu.VMEM((tm, D_MODEL), BF16),
            pltpu.VMEM((2 * POOL_HEAD, COL_BLOCK), F32),
            pltpu.VMEM((tm, COL_BLOCK), BF16),
            pltpu.VMEM((n_groups, POOL_HEAD, COL_BLOCK), F32),
        ]
        kern = functools.partial(_pool_in_prompt_kernel, tm=tm, tiles_per_batch=seq // tm)
    else:
        assert tm == m
        in_specs.append(pl.BlockSpec((n_batch, POOL_STATE, COL_BLOCK), lambda i, g: (0, 0, g)))
        args.append(state)
        scratch = [
            pltpu.VMEM((tm, D_MODEL), BF16),
            pltpu.VMEM((n_batch, POOL_HEAD + seq, COL_BLOCK), F32),
            pltpu.VMEM((tm, COL_BLOCK), F32),
        ]
        kern = functools.partial(_pool_in_sample_kernel, nb=n_batch, tb=seq)
    return pl.pallas_call(
        kern,
        grid=(m // tm, n_groups),
        in_specs=in_specs,
        out_specs=[
            pl.BlockSpec((tm, COL_BLOCK), lambda i, g: (i, g)),
            pl.BlockSpec((n_batch, POOL_STATE, D_INNER), lambda i, g: (0, 0, 0)),
        ],
        out_shape=[
            jax.ShapeDtypeStruct((m, D_INNER), BF16),
            jax.ShapeDtypeStruct((n_batch, POOL_STATE, D_INNER), F32),
        ],
        scratch_shapes=scratch,
        compiler_params=_params(("arbitrary", "arbitrary")),
        name="pool_in",
    )(*args)


def _out_proj_kernel(z_ref, w_ref, x_ref, o_ref):
    o_ref[...] = x_ref[...] + jnp.dot(z_ref[...], w_ref[...], preferred_element_type=F32)


def _out_proj(z, w, x2d, *, tm, tn):
    m, k = z.shape
    n = w.shape[1]
    return pl.pallas_call(
        _out_proj_kernel,
        grid=(m // tm, n // tn),
        in_specs=[
            pl.BlockSpec((tm, k), lambda i, j: (i, 0)),
            pl.BlockSpec((k, tn), lambda i, j: (0, j)),
            pl.BlockSpec((tm, tn), lambda i, j: (i, j)),
        ],
        out_specs=pl.BlockSpec((tm, tn), lambda i, j: (i, j)),
        out_shape=jax.ShapeDtypeStruct((m, n), F32),
        compiler_params=_params(("arbitrary", "arbitrary")),
        name="out_proj",
    )(z, w, x2d)


def _ln_out_proj_kernel(y_ref, gate_ref, lg_ref, lb_ref, w_ref, x_ref, o_ref, z_s):
    @pl.when(pl.program_id(1) == 0)
    def _():
        def rows(it, carry):
            r0 = pl.multiple_of(it * BF16_ROWS, BF16_ROWS)
            y = y_ref[pl.ds(r0, BF16_ROWS), :]
            yc = y - jnp.mean(y, axis=-1, keepdims=True)
            var = jnp.mean(yc * yc, axis=-1, keepdims=True)
            ln = yc * lax.rsqrt(var + NORM_EPS) * lg_ref[...] + lb_ref[...]
            gate = gate_ref[pl.ds(r0, BF16_ROWS), :].astype(F32)
            z_s[pl.ds(r0, BF16_ROWS), :] = (_silu(ln) * _silu(gate)).astype(BF16)
            return carry

        lax.fori_loop(0, z_s.shape[0] // BF16_ROWS, rows, 0, unroll=2)

    o_ref[...] = x_ref[...] + jnp.dot(z_s[...], w_ref[...], preferred_element_type=F32)


def _ln_out_proj(y, gate, ln_g, ln_b, w, x2d, *, tm, tn):
    m, k = y.shape
    n = w.shape[1]
    return pl.pallas_call(
        _ln_out_proj_kernel,
        grid=(m // tm, n // tn),
        in_specs=[
            pl.BlockSpec((tm, k), lambda i, j: (i, 0)),
            pl.BlockSpec((tm, k), lambda i, j: (i, 0)),
            pl.BlockSpec((1, k), lambda i, j: (0, 0)),
            pl.BlockSpec((1, k), lambda i, j: (0, 0)),
            pl.BlockSpec((k, tn), lambda i, j: (0, j)),
            pl.BlockSpec((tm, tn), lambda i, j: (i, j)),
        ],
        out_specs=pl.BlockSpec((tm, tn), lambda i, j: (i, j)),
        out_shape=jax.ShapeDtypeStruct((m, n), F32),
        scratch_shapes=[pltpu.VMEM((tm, k), BF16)],
        compiler_params=_params(("arbitrary", "arbitrary")),
        name="ln_out_proj",
    )(y, gate, ln_g.reshape(1, k), ln_b.reshape(1, k), w, x2d)


def _head_norm(y, g128):
    lane = lax.broadcasted_iota(jnp.int32, (1, LANE), 1)
    low = lane < ATT_HEAD_DIM
    outs = []
    for c in range(y.shape[1] // LANE):
        blk = y[:, c * LANE:(c + 1) * LANE]
        sq = blk * blk
        s_lo = jnp.sum(jnp.where(low, sq, 0.0), axis=-1, keepdims=True)
        s_hi = jnp.sum(jnp.where(low, 0.0, sq), axis=-1, keepdims=True)
        ms = jnp.where(low, s_lo, s_hi) * (1.0 / ATT_HEAD_DIM)
        outs.append(blk * lax.rsqrt(ms + NORM_EPS) * g128)
    return jnp.concatenate(outs, axis=1)


def _attn_in_kernel(x_ref, ng_ref, w_ref, wkt_ref, qg_ref, kg_ref, q_ref, kt_ref, v_ref, vn_ref, gate_ref, h_s,
                    *, q_dtype, v_dtype, tm):
    j = pl.program_id(1)

    @pl.when(j == 0)
    def _():
        _rms_rows(x_ref, ng_ref, h_s)

    @pl.when(j < 2)
    def _():
        y = jnp.dot(h_s[...], w_ref[...], preferred_element_type=F32)
        q_ref[...] = (_head_norm(y, qg_ref[...]) * QK_SCALE).astype(q_dtype)

    @pl.when(jnp.logical_and(j >= 2, j < 4))
    def _():
        yt = lax.dot_general(wkt_ref[...], h_s[...], (((1,), (1,)), ((), ())), preferred_element_type=F32)
        y3 = yt.reshape(COL_BLOCK // ATT_HEAD_DIM, ATT_HEAD_DIM, tm)
        ms = jnp.mean(y3 * y3, axis=1, keepdims=True)
        kt_ref[...] = (y3 * lax.rsqrt(ms + NORM_EPS) * kg_ref[...][None]).reshape(COL_BLOCK, tm)

    for jj in (4, 5):
        @pl.when(j == jj)
        def _(jj=jj):
            y = jnp.dot(h_s[...], w_ref[...], preferred_element_type=F32)
            v_ref[...] = y.astype(v_dtype)
            if vn_ref is not None:
                for hl in range(COL_BLOCK // LANE):
                    hh = (jj - 4) * (COL_BLOCK // LANE) + hl
                    vn_ref[pl.ds(hh, tm, stride=ATT_HEADS), :] = y[:, hl * LANE:(hl + 1) * LANE]

    @pl.when(j >= 6)
    def _():
        gate_ref[...] = jnp.dot(h_s[...], w_ref[...], preferred_element_type=F32)


def _attn_in(x2d, norm_g, w_in, w_kt, q_norm_g, k_norm_g, *, n_batch, seq, tm, q_dtype, v_dtype, native_v):
    m = x2d.shape[0]
    tiles_per_batch = seq // tm
    n_col = w_in.shape[1] // COL_BLOCK

    def out_spec(first):
        return pl.BlockSpec((tm, COL_BLOCK), lambda i, j: (i, jnp.clip(j - first, 0, 1)))

    out_specs = [
        out_spec(0),
        pl.BlockSpec((None, COL_BLOCK, tm),
                     lambda i, j: (i // tiles_per_batch, jnp.clip(j - 2, 0, 1), i % tiles_per_batch)),
        out_spec(4),
    ]
    out_shape = [
        jax.ShapeDtypeStruct((m, D_MODEL), q_dtype),
        jax.ShapeDtypeStruct((n_batch, D_MODEL, seq), F32),
        jax.ShapeDtypeStruct((m, D_MODEL), v_dtype),
    ]
    if native_v:
        out_specs.append(pl.BlockSpec((tm * ATT_HEADS, LANE), lambda i, j: (i, 0)))
        out_shape.append(jax.ShapeDtypeStruct((m * ATT_HEADS, LANE), F32))
    out_specs.append(out_spec(6))
    out_shape.append(jax.ShapeDtypeStruct((m, D_MODEL), F32))

    def kern(*refs):
        ins, outs, scratch = refs[:6], refs[6:-1], refs[-1]
        if native_v:
            q_ref, kt_ref, v_ref, vn_ref, gate_ref = outs
        else:
            (q_ref, kt_ref, v_ref, gate_ref), vn_ref = outs, None
        _attn_in_kernel(*ins, q_ref, kt_ref, v_ref, vn_ref, gate_ref, scratch,
                        q_dtype=q_dtype, v_dtype=v_dtype, tm=tm)

    return pl.pallas_call(
        kern,
        grid=(m // tm, n_col),
        in_specs=[
            pl.BlockSpec((tm, D_MODEL), lambda i, j: (i, 0)),
            pl.BlockSpec((1, D_MODEL), lambda i, j: (0, 0)),
            pl.BlockSpec((D_MODEL, COL_BLOCK), lambda i, j: (0, jnp.where(jnp.logical_and(j >= 2, j < 4), 1, j))),
            pl.BlockSpec((COL_BLOCK, D_MODEL), lambda i, j: (jnp.clip(j - 2, 0, 1), 0)),
            pl.BlockSpec((1, LANE), lambda i, j: (0, 0)),
            pl.BlockSpec((ATT_HEAD_DIM, 1), lambda i, j: (0, 0)),
        ],
        out_specs=out_specs,
        out_shape=out_shape,
        scratch_shapes=[pltpu.VMEM((tm, D_MODEL), BF16)],
        compiler_params=_params(("arbitrary", "arbitrary")),
        name="attn_in",
    )(x2d, norm_g.reshape(1, D_MODEL), w_in, w_kt, jnp.tile(q_norm_g, 2).reshape(1, LANE),
      k_norm_g.reshape(ATT_HEAD_DIM, 1))


def _lambda(lq1_ref, lk1_ref, lq2_ref, lk2_ref):
    a = jnp.sum(lq1_ref[...] * lk1_ref[...], axis=-1, keepdims=True)
    b = jnp.sum(lq2_ref[...] * lk2_ref[...], axis=-1, keepdims=True)
    return jnp.exp(a) - jnp.exp(b) + LAMBDA_INIT


def _sub_ln_gate(o, subln_g, gate):
    ms = jnp.mean(o * o, axis=-1, keepdims=True)
    o = o * lax.rsqrt(ms + NORM_EPS) * subln_g * (1.0 - LAMBDA_INIT)
    return o * _silu(gate)


def _attn_prompt_kernel(q_ref, kt_ref, v_ref, gate_ref, sg_ref, lq1_ref, lk1_ref, lq2_ref, lk2_ref,
                        z_ref, ka_s, m_s, l_s, acc_s, *, tq, tk):
    h = pl.program_id(1)
    qi = pl.program_id(2)
    rep = tk // LANE
    seq = kt_ref.shape[1]
    slope2 = jnp.exp2(-0.5 * jnp.full((1, LANE), h + 1, jnp.int32).astype(F32)) * LOG2E

    @pl.when(qi == 0)
    def _():
        col = lax.broadcasted_iota(jnp.int32, (BF16_ROWS, seq), 1) % tk
        row = lax.broadcasted_iota(jnp.int32, (BF16_ROWS, seq), 0)
        b = pltpu.repeat(slope2, seq // LANE, axis=1) * col.astype(F32)
        hi = b.astype(BF16).astype(F32)
        mid = (b - hi).astype(BF16).astype(F32)
        lo = b - hi - mid
        rows = jnp.where(row == 0, hi, jnp.where(row == 1, mid, jnp.where(row == 2, lo, 0.0))).astype(BF16)
        zeros = jnp.zeros((ATT_HEAD_DIM - BF16_ROWS, seq), BF16)
        ka_s[0, 0:ATT_HEAD_DIM, :] = kt_ref[0:ATT_HEAD_DIM, :].astype(BF16)
        ka_s[0, ATT_HEAD_DIM:ATT_HEAD_DIM + BF16_ROWS, :] = rows
        ka_s[0, ATT_HEAD_DIM + BF16_ROWS:, :] = zeros
        ka_s[1, 0:BF16_ROWS, :] = rows
        ka_s[1, BF16_ROWS:ATT_HEAD_DIM, :] = zeros
        ka_s[1, ATT_HEAD_DIM:, :] = kt_ref[ATT_HEAD_DIM:, :].astype(BF16)

    q = q_ref[...]
    lane = lax.broadcasted_iota(jnp.int32, (1, LANE), 1)
    low = lane < ATT_HEAD_DIM
    zero = jnp.zeros_like(q)
    ones1 = jnp.where(jnp.logical_and(lane >= ATT_HEAD_DIM, lane < ATT_HEAD_DIM + 3), 1.0, 0.0).astype(BF16)
    ones2 = jnp.where(lane < 3, 1.0, 0.0).astype(BF16)
    qa = (jnp.where(low, q, zero) + ones1, jnp.where(low, zero, q) + ones2)

    m_s[...] = jnp.full(m_s.shape, NEG_INF, F32)
    l_s[...] = jnp.zeros(l_s.shape, F32)
    acc_s[...] = jnp.zeros(acc_s.shape, F32)

    def tile(kj, masked):
        start = pl.multiple_of(kj * tk, tk)
        vt = v_ref[pl.ds(start, tk), :]
        shift = slope2 * jnp.full((1, LANE), kj * tk - qi * tq, jnp.int32).astype(F32)
        for c in range(2):
            s = jnp.dot(qa[c], ka_s[c, :, pl.ds(start, tk)], preferred_element_type=F32)
            if masked:
                row_i = lax.broadcasted_iota(jnp.int32, (tq, tk), 0)
                col_j = lax.broadcasted_iota(jnp.int32, (tq, tk), 1) + (kj * tk - qi * tq)
                s = jnp.where(col_j <= row_i, s, NEG_INF)
            m_old = m_s[c]
            m_new = jnp.maximum(m_old, jnp.max(s, axis=-1, keepdims=True) + shift)
            alpha = jnp.exp2(m_old - m_new)
            p = jnp.exp2(s - pltpu.repeat(m_new - shift, rep, axis=1))
            l_s[c] = alpha * l_s[c] + jnp.sum(p, axis=-1, keepdims=True)
            acc_s[c] = alpha * acc_s[c] + jnp.dot(p.astype(BF16), vt, preferred_element_type=F32)
            m_s[c] = m_new

    n_full = (qi * tq) // tk
    n_diag = tq // tk

    def body(kj, carry):
        tile(kj, False)
        return carry

    lax.fori_loop(0, n_full, body, 0)
    for d in range(n_diag):
        tile(n_full + d, True)

    lam = _lambda(lq1_ref, lk1_ref, lq2_ref, lk2_ref)
    o = acc_s[0] / l_s[0] - lam * (acc_s[1] / l_s[1])
    z_ref[...] = _sub_ln_gate(o, sg_ref[...], gate_ref[...]).astype(BF16)


def _attn_prompt(q, kt, v, gate, subln_g, lams, *, n_batch, seq, tq, tk):
    assert tq % tk == 0 and seq % tq == 0
    vec = pl.BlockSpec((1, ATT_HEAD_DIM), lambda b, h, i: (0, 0))
    kern = functools.partial(_attn_prompt_kernel, tq=tq, tk=tk)
    return pl.pallas_call(
        kern,
        grid=(n_batch, ATT_HEADS, seq // tq),
        in_specs=[
            pl.BlockSpec((None, tq, LANE), lambda b, h, i: (b, i, h)),
            pl.BlockSpec((None, LANE, seq), lambda b, h, i: (b, h, 0)),
            pl.BlockSpec((None, seq, LANE), lambda b, h, i: (b, 0, h)),
            pl.BlockSpec((None, tq, LANE), lambda b, h, i: (b, i, h)),
            pl.BlockSpec((1, LANE), lambda b, h, i: (0, 0)),
            vec, vec, vec, vec,
        ],
        out_specs=pl.BlockSpec((None, tq, LANE), lambda b, h, i: (b, i, h)),
        out_shape=jax.ShapeDtypeStruct((n_batch, seq, D_MODEL), BF16),
        scratch_shapes=[
            pltpu.VMEM((2, LANE, seq), BF16),
            pltpu.VMEM((2, tq, LANE), F32),
            pltpu.VMEM((2, tq, LANE), F32),
            pltpu.VMEM((2, tq, LANE), F32),
        ],
        compiler_params=_params(("arbitrary", "arbitrary", "arbitrary")),
        name="attn_prompt",
    )(q, kt, v, gate, subln_g.reshape(1, LANE), *[t.reshape(1, ATT_HEAD_DIM) for t in lams])


def _attn_sample_kernel(pt_ref, q_ref, knt_ref, vn_ref, gate_ref, sg_ref,
                        lq1_ref, lk1_ref, lq2_ref, lk2_ref, *rest, pages, n_steps, past_len, t_new):
    k_refs = rest[:pages]
    v_refs = rest[pages:2 * pages]
    z_ref, q3_s, vh_s, slope_s, m_s, l_s, acc_s = rest[2 * pages:]
    b = pl.program_id(0)
    p = pl.program_id(1)
    n_rows = 2 * t_new
    shape3 = (ATT_HEADS, n_rows, LANE)

    @pl.when(p == 0)
    def _():
        row_c = lax.broadcasted_iota(jnp.int32, (n_rows, LANE), 0) // t_new
        lane_c = lax.broadcasted_iota(jnp.int32, (n_rows, LANE), 1) // ATT_HEAD_DIM
        for hh in range(ATT_HEADS):
            blk = q_ref[:, hh * LANE:(hh + 1) * LANE]
            both = jnp.concatenate([blk, blk], axis=0)
            q3_s[hh] = jnp.where(row_c == lane_c, both, 0.0).astype(BF16)
        head = lax.broadcasted_iota(jnp.int32, shape3, 0)
        slope_s[...] = jnp.exp2(-0.5 * (head + 1).astype(F32)) * LOG2E
        m_s[...] = jnp.full(shape3, NEG_INF, F32)
        l_s[...] = jnp.zeros(shape3, F32)
        acc_s[...] = jnp.zeros(shape3, F32)

    q3 = q3_s[...]

    def update(s, vb):
        m_old = m_s[...]
        m_new = jnp.maximum(m_old, jnp.max(s, axis=-1, keepdims=True))
        alpha = jnp.exp2(m_old - m_new)
        t = s.shape[-1]
        m_all = pltpu.repeat(m_new, t // LANE, axis=2) if t % LANE == 0 else m_new[:, :, 0:t]
        pr = jnp.exp2(s - m_all)
        l_s[...] = alpha * l_s[...] + jnp.sum(pr, axis=-1, keepdims=True)
        pv = jnp.einsum("hrt,hte->hre", pr.astype(BF16), vb, preferred_element_type=F32)
        acc_s[...] = alpha * acc_s[...] + pv
        m_s[...] = m_new

    tok = lax.broadcasted_iota(jnp.int32, (1, 1, PAGE_SIZE), 2)
    slope = slope_s[...]
    scores = []
    for r in range(pages):
        k3 = k_refs[r][...].reshape(ATT_HEADS, LANE, PAGE_SIZE).astype(BF16)
        k_pos = ((p * pages + r) * PAGE_SIZE - past_len + tok).astype(F32)
        scores.append(jnp.einsum("hrk,hkt->hrt", q3, k3, preferred_element_type=F32) + slope * k_pos)
        v3 = v_refs[r][...].reshape(PAGE_SIZE, ATT_HEADS, LANE)
        vh_s[:, r * PAGE_SIZE:(r + 1) * PAGE_SIZE, :] = jnp.swapaxes(v3, 0, 1).astype(BF16)
    update(jnp.concatenate(scores, axis=-1), vh_s[...])

    @pl.when(p == n_steps - 1)
    def _():
        n_tok = knt_ref.shape[1]
        k3 = knt_ref[...].reshape(ATT_HEADS, LANE, n_tok).astype(BF16)
        tok_n = lax.broadcasted_iota(jnp.int32, (1, n_rows, n_tok), 2)
        row_t = lax.broadcasted_iota(jnp.int32, (1, n_rows, n_tok), 1) % t_new
        valid = jnp.logical_and(tok_n // t_new == b, tok_n % t_new <= row_t)
        s = jnp.einsum("hrk,hkt->hrt", q3, k3, preferred_element_type=F32)
        s = s + slope[:, :, 0:n_tok] * (tok_n % t_new).astype(F32)
        s = jnp.where(valid, s, NEG_INF)
        vb = jnp.stack([vn_ref[:, hh * LANE:(hh + 1) * LANE] for hh in range(ATT_HEADS)], axis=0).astype(BF16)
        update(s, vb)

        lam = _lambda(lq1_ref, lk1_ref, lq2_ref, lk2_ref)
        acc = acc_s[...]
        l = l_s[...]
        o = acc[:, 0:t_new, :] / l[:, 0:t_new, :] - lam * (acc[:, t_new:, :] / l[:, t_new:, :])
        for hh in range(ATT_HEADS):
            gate = gate_ref[:, hh * LANE:(hh + 1) * LANE]
            z_ref[:, hh * LANE:(hh + 1) * LANE] = _sub_ln_gate(o[hh], sg_ref[...], gate)


def _attn_sample(q, knt, v_new, gate, cache_kt, cache_v2, page_table, subln_g, lams, *, pages):
    n_batch, t_new, _ = q.shape
    n_pages = page_table.shape[1]
    n_steps = n_pages // pages
    n_rows = 2 * t_new
    n_tok = n_batch * t_new

    tok_spec = pl.BlockSpec((None, t_new, D_MODEL), lambda b, p, pt: (b, 0, 0))
    vec = pl.BlockSpec((1, ATT_HEAD_DIM), lambda b, p, pt: (0, 0))

    def page_spec(r):
        return pl.BlockSpec((None, D_MODEL, PAGE_SIZE), lambda b, p, pt: (pt[b, p * pages + r], 0, 0))

    kern = functools.partial(_attn_sample_kernel, pages=pages, n_steps=n_steps,
                             past_len=n_pages * PAGE_SIZE, t_new=t_new)
    grid_spec = pltpu.PrefetchScalarGridSpec(
        num_scalar_prefetch=1,
        grid=(n_batch, n_steps),
        in_specs=[tok_spec,
                  pl.BlockSpec((D_MODEL, n_tok), lambda b, p, pt: (0, 0)),
                  pl.BlockSpec((n_tok, D_MODEL), lambda b, p, pt: (0, 0)),
                  tok_spec,
                  pl.BlockSpec((1, LANE), lambda b, p, pt: (0, 0)), vec, vec, vec, vec]
        + [page_spec(r) for r in range(pages)] + [page_spec(r) for r in range(pages)],
        out_specs=tok_spec,
        scratch_shapes=[
            pltpu.VMEM((ATT_HEADS, n_rows, LANE), BF16),
            pltpu.VMEM((ATT_HEADS, pages * PAGE_SIZE, LANE), BF16),
            pltpu.VMEM((ATT_HEADS, n_rows, LANE), F32),
            pltpu.VMEM((ATT_HEADS, n_rows, LANE), F32),
            pltpu.VMEM((ATT_HEADS, n_rows, LANE), F32),
            pltpu.VMEM((ATT_HEADS, n_rows, LANE), F32),
        ],
    )
    return pl.pallas_call(
        kern,
        grid_spec=grid_spec,
        out_shape=jax.ShapeDtypeStruct((n_batch, t_new, D_MODEL), F32),
        compiler_params=_params(("arbitrary", "arbitrary")),
        name="attn_sample",
    )(page_table, q, knt, v_new, gate, subln_g.reshape(1, LANE),
      *[t.reshape(1, ATT_HEAD_DIM) for t in lams], *([cache_kt] * pages), *([cache_v2] * pages))


def _conv_in_kernel(*refs, nb, tb, tiles_per_batch, has_state):
    if has_state:
        (x_ref, ng_ref, wa_ref, wl_ref, wg_ref, cw_ref, cb_ref, st_in_ref,
         y_ref, gate_ref, st_ref, h_s, ext_s, wb_s) = refs
        carry_s = None
    else:
        (x_ref, ng_ref, wa_ref, wl_ref, wg_ref, cw_ref, cb_ref,
         y_ref, gate_ref, st_ref, h_s, ext_s, wb_s, carry_s) = refs
    i = pl.program_id(0)
    cb = pl.program_id(1)

    @pl.when(cb == 0)
    def _():
        _rms_rows(x_ref, ng_ref, h_s)

    a = jnp.dot(h_s[...], wa_ref[...], preferred_element_type=F32)
    glu = jnp.dot(h_s[...], wl_ref[...], preferred_element_type=F32)
    c = a * jax.nn.sigmoid(glu)

    head0 = CONV_HEAD - CONV_STATE
    if has_state:
        ext_s[:, 0:head0, :] = jnp.zeros((nb, head0, COL_BLOCK), F32)
        ext_s[:, head0:CONV_HEAD, :] = st_in_ref[...]
    else:
        first = (i % tiles_per_batch) == 0

        @pl.when(first)
        def _():
            ext_s[0, 0:CONV_HEAD, :] = jnp.zeros((CONV_HEAD, COL_BLOCK), F32)

        @pl.when(jnp.logical_not(first))
        def _():
            ext_s[0, 0:CONV_HEAD, :] = carry_s[cb]

    for b in range(nb):
        ext_s[b, CONV_HEAD:CONV_HEAD + tb, :] = c[b * tb:(b + 1) * tb, :]

    rc = min(tb, ROW_CHUNK)
    n_lane = COL_BLOCK // LANE_CHUNK
    n_row = tb // rc

    for j in range(CONV_WIDTH):
        wb_s[j] = jnp.broadcast_to(cw_ref[j:j + 1, :], (SUBLANES, COL_BLOCK))

    def chunk(it, carry):
        b = it // (n_row * n_lane)
        r0 = pl.multiple_of(((it // n_lane) % n_row) * rc, rc)
        l0 = pl.multiple_of((it % n_lane) * LANE_CHUNK, LANE_CHUNK)
        blk = ext_s[b, pl.ds(r0, rc + CONV_HEAD), pl.ds(l0, LANE_CHUNK)]
        acc = jnp.broadcast_to(cb_ref[:, pl.ds(l0, LANE_CHUNK)], (rc, LANE_CHUNK))
        for r in range(SUBLANES):
            shifted = blk if r == 0 else blk[r:r + rc + CONV_HEAD - SUBLANES, :]
            for j in range(CONV_WIDTH):
                if (head0 + j) % SUBLANES == r:
                    a = head0 + j - r
                    w_j = pltpu.repeat(wb_s[j, :, pl.ds(l0, LANE_CHUNK)], rc // SUBLANES, axis=0)
                    acc = acc + w_j * shifted[a:a + rc, :]
        y_ref[pl.ds(b * tb + r0, rc), pl.ds(l0, LANE_CHUNK)] = acc
        return carry

    lax.fori_loop(0, nb * n_row * n_lane, chunk, 0, unroll=2)
    col0 = pl.multiple_of(cb * COL_BLOCK, COL_BLOCK)
    if has_state:
        for b in range(nb):
            st_ref[b, :, pl.ds(col0, COL_BLOCK)] = ext_s[b, tb + head0:tb + CONV_HEAD, :]
    else:
        @pl.when((i % tiles_per_batch) == tiles_per_batch - 1)
        def _():
            st_ref[i // tiles_per_batch, :, pl.ds(col0, COL_BLOCK)] = ext_s[0, tb + head0:tb + CONV_HEAD, :]

        carry_s[cb] = ext_s[0, tb:tb + CONV_HEAD, :]

    gate_ref[...] = jnp.dot(h_s[...], wg_ref[...], preferred_element_type=F32).astype(BF16)


def _conv_in(x2d, norm_g, w_in, conv_w, conv_b, state, *, n_batch, seq, tm):
    m = x2d.shape[0]
    has_state = state is not None
    if has_state:
        assert tm == m
        nb, tb, tiles_per_batch = n_batch, seq, 1
    else:
        assert seq % tm == 0 and tm >= CONV_HEAD
        nb, tb, tiles_per_batch = 1, tm, seq // tm
    n_cb = D_INNER // COL_BLOCK
    in_specs = [
        pl.BlockSpec((tm, D_MODEL), lambda i, c: (i, 0)),
        pl.BlockSpec((1, D_MODEL), lambda i, c: (0, 0)),
        pl.BlockSpec((D_MODEL, COL_BLOCK), lambda i, c: (0, c)),
        pl.BlockSpec((D_MODEL, COL_BLOCK), lambda i, c: (0, n_cb + c)),
        pl.BlockSpec((D_MODEL, COL_BLOCK), lambda i, c: (0, 2 * n_cb + c)),
        pl.BlockSpec((CONV_WIDTH, COL_BLOCK), lambda i, c: (0, c)),
        pl.BlockSpec((1, COL_BLOCK), lambda i, c: (0, c)),
    ]
    args = [x2d, norm_g.reshape(1, D_MODEL), w_in, w_in, w_in, conv_w, conv_b.reshape(1, D_INNER)]
    scratch = [
        pltpu.VMEM((tm, D_MODEL), BF16),
        pltpu.VMEM((nb, CONV_HEAD + tb, COL_BLOCK), F32),
        pltpu.VMEM((CONV_WIDTH, SUBLANES, COL_BLOCK), F32),
    ]
    if has_state:
        in_specs.append(pl.BlockSpec((nb, CONV_STATE, COL_BLOCK), lambda i, c: (0, 0, c)))
        args.append(state)
    else:
        scratch.append(pltpu.VMEM((n_cb, CONV_HEAD, COL_BLOCK), F32))
    kern = functools.partial(_conv_in_kernel, nb=nb, tb=tb, tiles_per_batch=tiles_per_batch,
                             has_state=has_state)
    return pl.pallas_call(
        kern,
        grid=(m // tm, n_cb),
        in_specs=in_specs,
        out_specs=[
            pl.BlockSpec((tm, COL_BLOCK), lambda i, c: (i, c)),
            pl.BlockSpec((tm, COL_BLOCK), lambda i, c: (i, c)),
            pl.BlockSpec((n_batch, CONV_STATE, D_INNER), lambda i, c: (0, 0, 0)),
        ],
        out_shape=[
            jax.ShapeDtypeStruct((m, D_INNER), F32),
            jax.ShapeDtypeStruct((m, D_INNER), BF16),
            jax.ShapeDtypeStruct((n_batch, CONV_STATE, D_INNER), F32),
        ],
        scratch_shapes=scratch,
        compiler_params=_params(("arbitrary", "arbitrary")),
        name="conv_in",
    )(*args)


PROMPT_TM = 512
LN_TM = 512
LN_TN = 512
ATTN_TQ = 1024
ATTN_TK = 1024
SAMPLE_PAGES_PER_STEP = 8


def kernel(x_prompt, x_sample, state_pool_l0, cache_k_l1, cache_v_l1, state_conv_l2, state_pool_l3, page_table, norm_g_l0, w_in_l0, w_grp_l0, pool_scale_l0, w_out_l0, norm_g_l1, w_in_l1, q_norm_g_l1, k_norm_g_l1, lambda_q1_l1, lambda_k1_l1, lambda_q2_l1, lambda_k2_l1, subln_g_l1, w_out_l1, norm_g_l2, w_in_l2, conv_w_l2, conv_b_l2, ln_g_l2, ln_b_l2, w_out_l2, norm_g_l3, w_in_l3, w_grp_l3, pool_scale_l3, w_out_l3):
    nbp, seq, _ = x_prompt.shape
    nbs, t_new, _ = x_sample.shape
    mp, ms = nbp * seq, nbs * t_new
    xp = x_prompt.reshape(mp, D_MODEL)
    xs = x_sample.reshape(ms, D_MODEL)
    lams = (lambda_q1_l1, lambda_k1_l1, lambda_q2_l1, lambda_k2_l1)

    def bf(w):
        return w.astype(BF16)

    def pool_layer(xp, xs, state, norm_g, w_in, w_grp, scale, w_out):
        w_in, w_grp, w_out = bf(w_in), bf(w_grp), bf(w_out)
        zp, stp = _pool_in(xp, norm_g, w_in, w_grp, scale, None, n_batch=nbp, seq=seq, tm=PROMPT_TM)
        zs, sts = _pool_in(xs, norm_g, w_in, w_grp, scale, state, n_batch=nbs, seq=t_new, tm=ms)
        xp = _out_proj(zp, w_out, xp, tm=PROMPT_TM, tn=COL_BLOCK)
        xs = _out_proj(zs, w_out, xs, tm=ms, tn=COL_BLOCK)
        return xp, xs, stp, sts

    xp, xs, pool0_p, pool0_s = pool_layer(xp, xs, state_pool_l0, norm_g_l0, w_in_l0, w_grp_l0,
                                          pool_scale_l0, w_out_l0)

    w_in, w_out = bf(w_in_l1), bf(w_out_l1)
    w_kt = bf(lax.slice(w_in_l1, (0, D_MODEL), (D_MODEL, 2 * D_MODEL))).T
    qp, ktp, vp, vnp, gp = _attn_in(xp, norm_g_l1, w_in, w_kt, q_norm_g_l1, k_norm_g_l1, n_batch=nbp, seq=seq,
                                    tm=PROMPT_TM, q_dtype=BF16, v_dtype=BF16, native_v=True)
    qs, kts, vs, gs = _attn_in(xs, norm_g_l1, w_in, w_kt, q_norm_g_l1, k_norm_g_l1, n_batch=1, seq=ms,
                               tm=ms, q_dtype=F32, v_dtype=F32, native_v=False)
    shp = (nbp, seq, D_MODEL)
    zp = _attn_prompt(qp.reshape(shp), ktp, vp.reshape(shp), gp.reshape(shp), subln_g_l1, lams,
                      n_batch=nbp, seq=seq, tq=ATTN_TQ, tk=ATTN_TK)
    n_phys = cache_k_l1.shape[0]
    cache_kt = jnp.transpose(cache_k_l1, (0, 2, 3, 4, 1)).reshape(n_phys, D_MODEL, PAGE_SIZE)
    cache_v2 = cache_v_l1.reshape(n_phys, PAGE_SIZE * ATT_HEADS, ATT_VDIM)
    shs = (nbs, t_new, D_MODEL)
    zs = _attn_sample(qs.reshape(shs), kts[0], vs, gs.reshape(shs), cache_kt, cache_v2, page_table,
                      subln_g_l1, lams, pages=SAMPLE_PAGES_PER_STEP)
    xp = _out_proj(zp.reshape(mp, D_MODEL), w_out, xp, tm=PROMPT_TM, tn=COL_BLOCK)
    xs = _out_proj(zs.reshape(ms, D_MODEL).astype(BF16), w_out, xs, tm=ms, tn=COL_BLOCK)
    new_k_p = jnp.transpose(ktp.reshape(nbp, ATT_HEADS, 2, ATT_HEAD_DIM, seq), (0, 4, 1, 2, 3))
    new_v_p = vnp.reshape(nbp, seq, ATT_HEADS, ATT_VDIM)
    new_k_s = kts[0].T.reshape(nbs, t_new, ATT_HEADS, 2, ATT_HEAD_DIM)
    new_v_s = vs.reshape(nbs, t_new, ATT_HEADS, ATT_VDIM)

    w_in, w_out = bf(w_in_l2), bf(w_out_l2)
    yp, gp, conv_p = _conv_in(xp, norm_g_l2, w_in, conv_w_l2, conv_b_l2, None, n_batch=nbp, seq=seq,
                              tm=PROMPT_TM)
    ys, gs, conv_s = _conv_in(xs, norm_g_l2, w_in, conv_w_l2, conv_b_l2, state_conv_l2, n_batch=nbs,
                              seq=t_new, tm=ms)
    xp = _ln_out_proj(yp, gp, ln_g_l2, ln_b_l2, w_out, xp, tm=LN_TM, tn=LN_TN)
    xs = _ln_out_proj(ys, gs, ln_g_l2, ln_b_l2, w_out, xs, tm=ms, tn=LN_TN)

    xp, xs, pool3_p, pool3_s = pool_layer(xp, xs, state_pool_l3, norm_g_l3, w_in_l3, w_grp_l3,
                                          pool_scale_l3, w_out_l3)

    return (xp.reshape(nbp, seq, D_MODEL), xs.reshape(nbs, t_new, D_MODEL), pool0_p, pool0_s,
            new_k_p, new_v_p, new_k_s, new_v_s, conv_p, conv_s, pool3_p, pool3_s)
```

```python
import functools
import math

import jax
import jax.numpy as jnp
from jax import lax
from jax.experimental import pallas as pl
from jax.experimental.pallas import tpu as pltpu

F32 = jnp.float32
BF16 = jnp.bfloat16

D_MODEL = 2048
D_INNER = 4096
POOL_WINDOWS = (2, 4, 8, 16)
POOL_GROUP = 1024
POOL_STATE = 15
ATT_HEADS = 16
ATT_HEAD_DIM = 64
ATT_VDIM = 128
LAMBDA_INIT = 0.8 - 0.6 * math.exp(-0.3 * 1)
LOG2E = math.log2(math.e)
QK_SCALE = ATT_HEAD_DIM ** -0.5 * LOG2E
CONV_WIDTH = 31
CONV_STATE = 30
NORM_EPS = 1e-6
NEG_INF = -1e30
PAGE_SIZE = 128

VMEM_LIMIT_BYTES = 58 * 1024 * 1024
LANE = 128
SUBLANES = 8
BF16_ROWS = 16
COL_BLOCK = 1024
POOL_HEAD = 16
CONV_HEAD = 32
ROW_CHUNK = 32
LANE_CHUNK = 256


def _params(sem):
    return pltpu.CompilerParams(dimension_semantics=sem, vmem_limit_bytes=VMEM_LIMIT_BYTES)


def _rms_rows(x_ref, g_ref, h_ref):
    xf = x_ref[...]
    ms = jnp.mean(xf * xf, axis=-1, keepdims=True)
    h_ref[...] = (xf * lax.rsqrt(ms + NORM_EPS) * g_ref[...]).astype(BF16)


def _silu(x):
    return x * jax.nn.sigmoid(x)


def _pool_rows(blk, w, n_seen):
    rows = blk.shape[0] - POOL_HEAD
    cur = blk[POOL_HEAD:, :]
    acc = cur
    for j in range(1, w):
        acc = acc + blk[POOL_HEAD - j:POOL_HEAD - j + rows, :]
    cnt = jnp.minimum(n_seen, w).astype(F32)
    return jnp.where(n_seen >= w, acc * (1.0 / w), acc / cnt) - cur


def _pool_finish(h_s, pooled, wg_ref, wgrp_ref, sc_ref, z_ref):
    mixed = jnp.dot(pooled, wgrp_ref[0], preferred_element_type=F32) * sc_ref[...]
    gate = jnp.dot(h_s[...], wg_ref[...], preferred_element_type=F32)
    z_ref[...] = (mixed * _silu(gate)).astype(BF16)


def _pool_in_prompt_kernel(x_ref, ng_ref, wu_ref, wg_ref, wgrp_ref, sc_ref, pmat_ref,
                           z_ref, st_ref, h_s, edge_s, pooled_s, carry_s, *, tm, tiles_per_batch):
    i = pl.program_id(0)
    g = pl.program_id(1)

    @pl.when(g == 0)
    def _():
        _rms_rows(x_ref, ng_ref, h_s)

    u = jnp.dot(h_s[...], wu_ref[...], preferred_element_type=F32)
    first = (i % tiles_per_batch) == 0

    pooled_s[...] = jnp.dot(pmat_ref[0], u.astype(BF16), preferred_element_type=F32).astype(BF16)

    @pl.when(first)
    def _():
        edge_s[0:POOL_HEAD, :] = jnp.zeros((POOL_HEAD, COL_BLOCK), F32)

    @pl.when(jnp.logical_not(first))
    def _():
        edge_s[0:POOL_HEAD, :] = carry_s[g]

    edge_s[POOL_HEAD:, :] = u[0:POOL_HEAD, :]
    n_seen = ((i % tiles_per_batch) * tm + 1
              + lax.broadcasted_iota(jnp.int32, (POOL_HEAD, LANE_CHUNK), 0))
    for k, w in enumerate(POOL_WINDOWS):
        @pl.when(g == k)
        def _(w=w):
            for l0 in range(0, COL_BLOCK, LANE_CHUNK):
                pooled_s[0:POOL_HEAD, l0:l0 + LANE_CHUNK] = _pool_rows(
                    edge_s[:, l0:l0 + LANE_CHUNK], w, n_seen).astype(BF16)

    @pl.when((i % tiles_per_batch) == tiles_per_batch - 1)
    def _():
        col0 = pl.multiple_of(g * COL_BLOCK, COL_BLOCK)
        st_ref[i // tiles_per_batch, :, pl.ds(col0, COL_BLOCK)] = u[tm - POOL_STATE:tm, :]

    carry_s[g] = u[tm - POOL_HEAD:tm, :]
    _pool_finish(h_s, pooled_s[...], wg_ref, wgrp_ref, sc_ref, z_ref)


def _pool_in_sample_kernel(x_ref, ng_ref, wu_ref, wg_ref, wgrp_ref, sc_ref, st_in_ref,
                           z_ref, st_ref, h_s, ext_s, pooled_s, *, nb, tb):
    g = pl.program_id(1)

    @pl.when(g == 0)
    def _():
        _rms_rows(x_ref, ng_ref, h_s)

    u = jnp.dot(h_s[...], wu_ref[...], preferred_element_type=F32)

    ext_s[:, 0:1, :] = jnp.zeros((nb, 1, COL_BLOCK), F32)
    ext_s[:, 1:POOL_HEAD, :] = st_in_ref[...]
    for b in range(nb):
        ext_s[b, POOL_HEAD:, :] = u[b * tb:(b + 1) * tb, :]

    n_seen = POOL_STATE + 1 + lax.broadcasted_iota(jnp.int32, (tb, LANE_CHUNK), 0)
    for k, w in enumerate(POOL_WINDOWS):
        @pl.when(g == k)
        def _(w=w):
            for b in range(nb):
                for l0 in range(0, COL_BLOCK, LANE_CHUNK):
                    pooled_s[b * tb:(b + 1) * tb, l0:l0 + LANE_CHUNK] = _pool_rows(
                        ext_s[b, :, l0:l0 + LANE_CHUNK], w, n_seen)

    col0 = pl.multiple_of(g * COL_BLOCK, COL_BLOCK)
    for b in range(nb):
        st_ref[b, :, pl.ds(col0, COL_BLOCK)] = ext_s[b, tb + 1:tb + POOL_HEAD, :]
    _pool_finish(h_s, pooled_s[...].astype(BF16), wg_ref, wgrp_ref, sc_ref, z_ref)


def _pool_matrices(tm):
    t = jnp.arange(tm)[:, None]
    j = jnp.arange(tm)[None, :]
    mats = [jnp.where((t - j >= 0) & (t - j < w), 1.0 / w, 0.0) - (t == j) for w in POOL_WINDOWS]
    return jnp.stack(mats).astype(BF16)


def _pool_in(x2d, norm_g, w_in, w_grp, scale, state, *, n_batch, seq, tm):
    m = x2d.shape[0]
    n_groups = len(POOL_WINDOWS)
    in_specs = [
        pl.BlockSpec((tm, D_MODEL), lambda i, g: (i, 0)),
        pl.BlockSpec((1, D_MODEL), lambda i, g: (0, 0)),
        pl.BlockSpec((D_MODEL, COL_BLOCK), lambda i, g: (0, g)),
        pl.BlockSpec((D_MODEL, COL_BLOCK), lambda i, g: (0, n_groups + g)),
        pl.BlockSpec((1, POOL_GROUP, POOL_GROUP), lambda i, g: (g, 0, 0)),
        pl.BlockSpec((1, COL_BLOCK), lambda i, g: (0, g)),
    ]
    args = [x2d, norm_g.reshape(1, D_MODEL), w_in, w_in, w_grp, scale.reshape(1, D_INNER)]
    if state is None:
        assert seq % tm == 0 and tm >= POOL_HEAD
        in_specs.append(pl.BlockSpec((1, tm, tm), lambda i, g: (g, 0, 0)))
        args.append(_pool_matrices(tm))
        scratch = [
            pltpu.VMEM((tm, D_MODEL), BF16),
            pltpu.VMEM((2 * POOL_HEAD, COL_BLOCK), F32),
            pltpu.VMEM((tm, COL_BLOCK), BF16),
            pltpu.VMEM((n_groups, POOL_HEAD, COL_BLOCK), F32),
        ]
        kern = functools.partial(_pool_in_prompt_kernel, tm=tm, tiles_per_batch=seq // tm)
    else:
        assert tm == m
        in_specs.append(pl.BlockSpec((n_batch, POOL_STATE, COL_BLOCK), lambda i, g: (0, 0, g)))
        args.append(state)
        scratch = [
            pltpu.VMEM((tm, D_MODEL), BF16),
            pltpu.VMEM((n_batch, POOL_HEAD + seq, COL_BLOCK), F32),
            pltpu.VMEM((tm, COL_BLOCK), F32),
        ]
        kern = functools.partial(_pool_in_sample_kernel, nb=n_batch, tb=seq)
    return pl.pallas_call(
        kern,
        grid=(m // tm, n_groups),
        in_specs=in_specs,
        out_specs=[
            pl.BlockSpec((tm, COL_BLOCK), lambda i, g: (i, g)),
            pl.BlockSpec((n_batch, POOL_STATE, D_INNER), lambda i, g: (0, 0, 0)),
        ],
        out_shape=[
            jax.ShapeDtypeStruct((m, D_INNER), BF16),
            jax.ShapeDtypeStruct((n_batch, POOL_STATE, D_INNER), F32),
        ],
        scratch_shapes=scratch,
        compiler_params=_params(("arbitrary", "arbitrary")),
        name="pool_in",
    )(*args)


def _out_proj_kernel(z_ref, w_ref, x_ref, o_ref):
    o_ref[...] = x_ref[...] + jnp.dot(z_ref[...], w_ref[...], preferred_element_type=F32)


def _out_proj(z, w, x2d, *, tm, tn):
    m, k = z.shape
    n = w.shape[1]
    return pl.pallas_call(
        _out_proj_kernel,
        grid=(m // tm, n // tn),
        in_specs=[
            pl.BlockSpec((tm, k), lambda i, j: (i, 0)),
            pl.BlockSpec((k, tn), lambda i, j: (0, j)),
            pl.BlockSpec((tm, tn), lambda i, j: (i, j)),
        ],
        out_specs=pl.BlockSpec((tm, tn), lambda i, j: (i, j)),
        out_shape=jax.ShapeDtypeStruct((m, n), F32),
        compiler_params=_params(("arbitrary", "arbitrary")),
        name="out_proj",
    )(z, w, x2d)


def _ln_out_proj_kernel(y_ref, gate_ref, lg_ref, lb_ref, w_ref, x_ref, o_ref, z_s):
    @pl.when(pl.program_id(1) == 0)
    def _():
        def rows(it, carry):
            r0 = pl.multiple_of(it * BF16_ROWS, BF16_ROWS)
            y = y_ref[pl.ds(r0, BF16_ROWS), :]
            yc = y - jnp.mean(y, axis=-1, keepdims=True)
            var = jnp.mean(yc * yc, axis=-1, keepdims=True)
            ln = yc * lax.rsqrt(var + NORM_EPS) * lg_ref[...] + lb_ref[...]
            gate = gate_ref[pl.ds(r0, BF16_ROWS), :].astype(F32)
            z_s[pl.ds(r0, BF16_ROWS), :] = (_silu(ln) * _silu(gate)).astype(BF16)
            return carry

        lax.fori_loop(0, z_s.shape[0] // BF16_ROWS, rows, 0, unroll=2)

    o_ref[...] = x_ref[...] + jnp.dot(z_s[...], w_ref[...], preferred_element_type=F32)


def _ln_out_proj(y, gate, ln_g, ln_b, w, x2d, *, tm, tn):
    m, k = y.shape
    n = w.shape[1]
    return pl.pallas_call(
        _ln_out_proj_kernel,
        grid=(m // tm, n // tn),
        in_specs=[
            pl.BlockSpec((tm, k), lambda i, j: (i, 0)),
            pl.BlockSpec((tm, k), lambda i, j: (i, 0)),
            pl.BlockSpec((1, k), lambda i, j: (0, 0)),
            pl.BlockSpec((1, k), lambda i, j: (0, 0)),
            pl.BlockSpec((k, tn), lambda i, j: (0, j)),
            pl.BlockSpec((tm, tn), lambda i, j: (i, j)),
        ],
        out_specs=pl.BlockSpec((tm, tn), lambda i, j: (i, j)),
        out_shape=jax.ShapeDtypeStruct((m, n), F32),
        scratch_shapes=[pltpu.VMEM((tm, k), BF16)],
        compiler_params=_params(("arbitrary", "arbitrary")),
        name="ln_out_proj",
    )(y, gate, ln_g.reshape(1, k), ln_b.reshape(1, k), w, x2d)


def _head_norm(y, g128):
    lane = lax.broadcasted_iota(jnp.int32, (1, LANE), 1)
    low = lane < ATT_HEAD_DIM
    outs = []
    for c in range(y.shape[1] // LANE):
        blk = y[:, c * LANE:(c + 1) * LANE]
        sq = blk * blk
        s_lo = jnp.sum(jnp.where(low, sq, 0.0), axis=-1, keepdims=True)
        s_hi = jnp.sum(jnp.where(low, 0.0, sq), axis=-1, keepdims=True)
        ms = jnp.where(low, s_lo, s_hi) * (1.0 / ATT_HEAD_DIM)
        outs.append(blk * lax.rsqrt(ms + NORM_EPS) * g128)
    return jnp.concatenate(outs, axis=1)


def _attn_in_kernel(x_ref, ng_ref, w_ref, wkt_ref, qg_ref, kg_ref, q_ref, kt_ref, v_ref, vn_ref, gate_ref, h_s,
                    *, q_dtype, v_dtype, tm):
    j = pl.program_id(1)

    @pl.when(j == 0)
    def _():
        _rms_rows(x_ref, ng_ref, h_s)

    @pl.when(j < 2)
    def _():
        y = jnp.dot(h_s[...], w_ref[...], preferred_element_type=F32)
        q_ref[...] = (_head_norm(y, qg_ref[...]) * QK_SCALE).astype(q_dtype)

    @pl.when(jnp.logical_and(j >= 2, j < 4))
    def _():
        yt = lax.dot_general(wkt_ref[...], h_s[...], (((1,), (1,)), ((), ())), preferred_element_type=F32)
        y3 = yt.reshape(COL_BLOCK // ATT_HEAD_DIM, ATT_HEAD_DIM, tm)
        ms = jnp.mean(y3 * y3, axis=1, keepdims=True)
        kt_ref[...] = (y3 * lax.rsqrt(ms + NORM_EPS) * kg_ref[...][None]).reshape(COL_BLOCK, tm)

    for jj in (4, 5):
        @pl.when(j == jj)
        def _(jj=jj):
            y = jnp.dot(h_s[...], w_ref[...], preferred_element_type=F32)
            v_ref[...] = y.astype(v_dtype)
            if vn_ref is not None:
                for hl in range(COL_BLOCK // LANE):
                    hh = (jj - 4) * (COL_BLOCK // LANE) + hl
                    vn_ref[pl.ds(hh, tm, stride=ATT_HEADS), :] = y[:, hl * LANE:(hl + 1) * LANE]

    @pl.when(j >= 6)
    def _():
        gate_ref[...] = jnp.dot(h_s[...], w_ref[...], preferred_element_type=F32)


def _attn_in(x2d, norm_g, w_in, w_kt, q_norm_g, k_norm_g, *, n_batch, seq, tm, q_dtype, v_dtype, native_v):
    m = x2d.shape[0]
    tiles_per_batch = seq // tm
    n_col = w_in.shape[1] // COL_BLOCK

    def out_spec(first):
        return pl.BlockSpec((tm, COL_BLOCK), lambda i, j: (i, jnp.clip(j - first, 0, 1)))

    out_specs = [
        out_spec(0),
        pl.BlockSpec((None, COL_BLOCK, tm),
                     lambda i, j: (i // tiles_per_batch, jnp.clip(j - 2, 0, 1), i % tiles_per_batch)),
        out_spec(4),
    ]
    out_shape = [
        jax.ShapeDtypeStruct((m, D_MODEL), q_dtype),
        jax.ShapeDtypeStruct((n_batch, D_MODEL, seq), F32),
        jax.ShapeDtypeStruct((m, D_MODEL), v_dtype),
    ]
    if native_v:
        out_specs.append(pl.BlockSpec((tm * ATT_HEADS, LANE), lambda i, j: (i, 0)))
        out_shape.append(jax.ShapeDtypeStruct((m * ATT_HEADS, LANE), F32))
    out_specs.append(out_spec(6))
    out_shape.append(jax.ShapeDtypeStruct((m, D_MODEL), F32))

    def kern(*refs):
        ins, outs, scratch = refs[:6], refs[6:-1], refs[-1]
        if native_v:
            q_ref, kt_ref, v_ref, vn_ref, gate_ref = outs
        else:
            (q_ref, kt_ref, v_ref, gate_ref), vn_ref = outs, None
        _attn_in_kernel(*ins, q_ref, kt_ref, v_ref, vn_ref, gate_ref, scratch,
                        q_dtype=q_dtype, v_dtype=v_dtype, tm=tm)

    return pl.pallas_call(
        kern,
        grid=(m // tm, n_col),
        in_specs=[
            pl.BlockSpec((tm, D_MODEL), lambda i, j: (i, 0)),
            pl.BlockSpec((1, D_MODEL), lambda i, j: (0, 0)),
            pl.BlockSpec((D_MODEL, COL_BLOCK), lambda i, j: (0, jnp.where(jnp.logical_and(j >= 2, j < 4), 1, j))),
            pl.BlockSpec((COL_BLOCK, D_MODEL), lambda i, j: (jnp.clip(j - 2, 0, 1), 0)),
            pl.BlockSpec((1, LANE), lambda i, j: (0, 0)),
            pl.BlockSpec((ATT_HEAD_DIM, 1), lambda i, j: (0, 0)),
        ],
        out_specs=out_specs,
        out_shape=out_shape,
        scratch_shapes=[pltpu.VMEM((tm, D_MODEL), BF16)],
        compiler_params=_params(("arbitrary", "arbitrary")),
        name="attn_in",
    )(x2d, norm_g.reshape(1, D_MODEL), w_in, w_kt, jnp.tile(q_norm_g, 2).reshape(1, LANE),
      k_norm_g.reshape(ATT_HEAD_DIM, 1))


def _lambda(lq1_ref, lk1_ref, lq2_ref, lk2_ref):
    a = jnp.sum(lq1_ref[...] * lk1_ref[...], axis=-1, keepdims=True)
    b = jnp.sum(lq2_ref[...] * lk2_ref[...], axis=-1, keepdims=True)
    return jnp.exp(a) - jnp.exp(b) + LAMBDA_INIT


def _sub_ln_gate(o, subln_g, gate):
    ms = jnp.mean(o * o, axis=-1, keepdims=True)
    o = o * lax.rsqrt(ms + NORM_EPS) * subln_g * (1.0 - LAMBDA_INIT)
    return o * _silu(gate)


def _attn_prompt_kernel(q_ref, kt_ref, v_ref, gate_ref, sg_ref, lq1_ref, lk1_ref, lq2_ref, lk2_ref,
                        z_ref, ka_s, m_s, l_s, acc_s, *, tq, tk):
    h = pl.program_id(1)
    qi = pl.program_id(2)
    seq = kt_ref.shape[1]
    slope2 = jnp.exp2(-0.5 * jnp.full((1, LANE), h + 1, jnp.int32).astype(F32)) * LOG2E

    @pl.when(qi == 0)
    def _():
        col = lax.broadcasted_iota(jnp.int32, (BF16_ROWS, seq), 1) % tk
        row = lax.broadcasted_iota(jnp.int32, (BF16_ROWS, seq), 0)
        b = pltpu.repeat(slope2, seq // LANE, axis=1) * col.astype(F32)
        hi = b.astype(BF16).astype(F32)
        mid = (b - hi).astype(BF16).astype(F32)
        lo = b - hi - mid
        rows = jnp.where(row == 0, hi, jnp.where(row == 1, mid, jnp.where(row == 2, lo, 0.0))).astype(BF16)
        zeros = jnp.zeros((ATT_HEAD_DIM - BF16_ROWS, seq), BF16)
        ka_s[0, 0:ATT_HEAD_DIM, :] = kt_ref[0:ATT_HEAD_DIM, :].astype(BF16)
        ka_s[0, ATT_HEAD_DIM:ATT_HEAD_DIM + BF16_ROWS, :] = rows
        ka_s[0, ATT_HEAD_DIM + BF16_ROWS:, :] = zeros
        ka_s[1, 0:BF16_ROWS, :] = rows
        ka_s[1, BF16_ROWS:ATT_HEAD_DIM, :] = zeros
        ka_s[1, ATT_HEAD_DIM:, :] = kt_ref[ATT_HEAD_DIM:, :].astype(BF16)

    q = q_ref[...]
    lane = lax.broadcasted_iota(jnp.int32, (1, LANE), 1)
    low = lane < ATT_HEAD_DIM
    zero = jnp.zeros_like(q)
    ones1 = jnp.where(jnp.logical_and(lane >= ATT_HEAD_DIM, lane < ATT_HEAD_DIM + 3), 1.0, 0.0).astype(BF16)
    ones2 = jnp.where(lane < 3, 1.0, 0.0).astype(BF16)
    qa = (jnp.where(low, q, zero) + ones1, jnp.where(low, zero, q) + ones2)

    m_s[...] = jnp.full(m_s.shape, NEG_INF, F32)
    l_s[...] = jnp.zeros(l_s.shape, F32)
    acc_s[...] = jnp.zeros(acc_s.shape, F32)

    def tile(kj, masked, r0=0, r1=tq, n_col=tk):
        start = pl.multiple_of(kj * tk, tk)
        vt = v_ref[pl.ds(start, n_col), :]
        shift = slope2 * jnp.full((1, LANE), kj * tk - qi * tq, jnp.int32).astype(F32)
        for c in range(2):
            s = jnp.dot(qa[c][r0:r1], ka_s[c, :, pl.ds(start, n_col)], preferred_element_type=F32)
            if masked:
                row_i = lax.broadcasted_iota(jnp.int32, (r1 - r0, n_col), 0) + r0
                col_j = lax.broadcasted_iota(jnp.int32, (r1 - r0, n_col), 1) + (kj * tk - qi * tq)
                s = jnp.where(col_j <= row_i, s, NEG_INF)
            m_old = m_s[c, r0:r1]
            m_new = jnp.maximum(m_old, jnp.max(s, axis=-1, keepdims=True) + shift)
            alpha = jnp.exp2(m_old - m_new)
            p = jnp.exp2(s - pltpu.repeat(m_new - shift, n_col // LANE, axis=1))
            l_s[c, r0:r1] = alpha * l_s[c, r0:r1] + jnp.sum(p, axis=-1, keepdims=True)
            acc_s[c, r0:r1] = alpha * acc_s[c, r0:r1] + jnp.dot(p.astype(BF16), vt, preferred_element_type=F32)
            m_s[c, r0:r1] = m_new

    n_full = (qi * tq) // tk

    def body(kj, carry):
        tile(kj, False)
        return carry

    lax.fori_loop(0, n_full, body, 0)
    if tq == tk:
        tile(n_full, True, 0, tq // 2, tk // 2)
        tile(n_full, True, tq // 2, tq, tk)
    else:
        for d in range(tq // tk):
            tile(n_full + d, True)

    lam = _lambda(lq1_ref, lk1_ref, lq2_ref, lk2_ref)
    o = acc_s[0] / l_s[0] - lam * (acc_s[1] / l_s[1])
    z_ref[...] = _sub_ln_gate(o, sg_ref[...], gate_ref[...]).astype(BF16)


def _attn_prompt(q, kt, v, gate, subln_g, lams, *, n_batch, seq, tq, tk):
    assert tq % tk == 0 and seq % tq == 0
    vec = pl.BlockSpec((1, ATT_HEAD_DIM), lambda b, h, i: (0, 0))
    kern = functools.partial(_attn_prompt_kernel, tq=tq, tk=tk)
    return pl.pallas_call(
        kern,
        grid=(n_batch, ATT_HEADS, seq // tq),
        in_specs=[
            pl.BlockSpec((None, tq, LANE), lambda b, h, i: (b, i, h)),
            pl.BlockSpec((None, LANE, seq), lambda b, h, i: (b, h, 0)),
            pl.BlockSpec((None, seq, LANE), lambda b, h, i: (b, 0, h)),
            pl.BlockSpec((None, tq, LANE), lambda b, h, i: (b, i, h)),
            pl.BlockSpec((1, LANE), lambda b, h, i: (0, 0)),
            vec, vec, vec, vec,
        ],
        out_specs=pl.BlockSpec((None, tq, LANE), lambda b, h, i: (b, i, h)),
        out_shape=jax.ShapeDtypeStruct((n_batch, seq, D_MODEL), BF16),
        scratch_shapes=[
            pltpu.VMEM((2, LANE, seq), BF16),
            pltpu.VMEM((2, tq, LANE), F32),
            pltpu.VMEM((2, tq, LANE), F32),
            pltpu.VMEM((2, tq, LANE), F32),
        ],
        compiler_params=_params(("arbitrary", "arbitrary", "arbitrary")),
        name="attn_prompt",
    )(q, kt, v, gate, subln_g.reshape(1, LANE), *[t.reshape(1, ATT_HEAD_DIM) for t in lams])


def _attn_sample_kernel(pt_ref, q_ref, knt_ref, vn_ref, gate_ref, sg_ref,
                        lq1_ref, lk1_ref, lq2_ref, lk2_ref, *rest, pages, n_steps, past_len, t_new):
    k_refs = rest[:pages]
    v_refs = rest[pages:2 * pages]
    z_ref, q3_s, vh_s, slope_s, m_s, l_s, acc_s = rest[2 * pages:]
    b = pl.program_id(0)
    p = pl.program_id(1)
    n_rows = 2 * t_new
    shape3 = (ATT_HEADS, n_rows, LANE)

    @pl.when(p == 0)
    def _():
        row_c = lax.broadcasted_iota(jnp.int32, (n_rows, LANE), 0) // t_new
        lane_c = lax.broadcasted_iota(jnp.int32, (n_rows, LANE), 1) // ATT_HEAD_DIM
        for hh in range(ATT_HEADS):
            blk = q_ref[:, hh * LANE:(hh + 1) * LANE]
            both = jnp.concatenate([blk, blk], axis=0)
            q3_s[hh] = jnp.where(row_c == lane_c, both, 0.0).astype(BF16)
        head = lax.broadcasted_iota(jnp.int32, shape3, 0)
        slope_s[...] = jnp.exp2(-0.5 * (head + 1).astype(F32)) * LOG2E
        m_s[...] = jnp.full(shape3, NEG_INF, F32)
        l_s[...] = jnp.zeros(shape3, F32)
        acc_s[...] = jnp.zeros(shape3, F32)

    q3 = q3_s[...]

    def update(s, vb):
        m_old = m_s[...]
        m_new = jnp.maximum(m_old, jnp.max(s, axis=-1, keepdims=True))
        alpha = jnp.exp2(m_old - m_new)
        t = s.shape[-1]
        m_all = pltpu.repeat(m_new, t // LANE, axis=2) if t % LANE == 0 else m_new[:, :, 0:t]
        pr = jnp.exp2(s - m_all)
        l_s[...] = alpha * l_s[...] + jnp.sum(pr, axis=-1, keepdims=True)
        pv = jnp.einsum("hrt,hte->hre", pr.astype(BF16), vb, preferred_element_type=F32)
        acc_s[...] = alpha * acc_s[...] + pv
        m_s[...] = m_new

    tok = lax.broadcasted_iota(jnp.int32, (1, 1, PAGE_SIZE), 2)
    slope = slope_s[...]
    scores = []
    for r in range(pages):
        k3 = k_refs[r][...].reshape(ATT_HEADS, LANE, PAGE_SIZE).astype(BF16)
        k_pos = ((p * pages + r) * PAGE_SIZE - past_len + tok).astype(F32)
        scores.append(jnp.einsum("hrk,hkt->hrt", q3, k3, preferred_element_type=F32) + slope * k_pos)
        v3 = v_refs[r][...].reshape(PAGE_SIZE, ATT_HEADS, LANE)
        vh_s[:, r * PAGE_SIZE:(r + 1) * PAGE_SIZE, :] = jnp.swapaxes(v3, 0, 1).astype(BF16)
    update(jnp.concatenate(scores, axis=-1), vh_s[...])

    @pl.when(p == n_steps - 1)
    def _():
        n_tok = knt_ref.shape[1]
        k3 = knt_ref[...].reshape(ATT_HEADS, LANE, n_tok).astype(BF16)
        tok_n = lax.broadcasted_iota(jnp.int32, (1, n_rows, n_tok), 2)
        row_t = lax.broadcasted_iota(jnp.int32, (1, n_rows, n_tok), 1) % t_new
        valid = jnp.logical_and(tok_n // t_new == b, tok_n % t_new <= row_t)
        s = jnp.einsum("hrk,hkt->hrt", q3, k3, preferred_element_type=F32)
        s = s + slope[:, :, 0:n_tok] * (tok_n % t_new).astype(F32)
        s = jnp.where(valid, s, NEG_INF)
        vb = jnp.stack([vn_ref[:, hh * LANE:(hh + 1) * LANE] for hh in range(ATT_HEADS)], axis=0).astype(BF16)
        update(s, vb)

        lam = _lambda(lq1_ref, lk1_ref, lq2_ref, lk2_ref)
        acc = acc_s[...]
        l = l_s[...]
        o = acc[:, 0:t_new, :] / l[:, 0:t_new, :] - lam * (acc[:, t_new:, :] / l[:, t_new:, :])
        for hh in range(ATT_HEADS):
            gate = gate_ref[:, hh * LANE:(hh + 1) * LANE]
            z_ref[:, hh * LANE:(hh + 1) * LANE] = _sub_ln_gate(o[hh], sg_ref[...], gate)


def _attn_sample(q, knt, v_new, gate, cache_kt, cache_v2, page_table, subln_g, lams, *, pages):
    n_batch, t_new, _ = q.shape
    n_pages = page_table.shape[1]
    n_steps = n_pages // pages
    n_rows = 2 * t_new
    n_tok = n_batch * t_new

    tok_spec = pl.BlockSpec((None, t_new, D_MODEL), lambda b, p, pt: (b, 0, 0))
    vec = pl.BlockSpec((1, ATT_HEAD_DIM), lambda b, p, pt: (0, 0))

    def page_spec(r):
        return pl.BlockSpec((None, D_MODEL, PAGE_SIZE), lambda b, p, pt: (pt[b, p * pages + r], 0, 0))

    kern = functools.partial(_attn_sample_kernel, pages=pages, n_steps=n_steps,
                             past_len=n_pages * PAGE_SIZE, t_new=t_new)
    grid_spec = pltpu.PrefetchScalarGridSpec(
        num_scalar_prefetch=1,
        grid=(n_batch, n_steps),
        in_specs=[tok_spec,
                  pl.BlockSpec((D_MODEL, n_tok), lambda b, p, pt: (0, 0)),
                  pl.BlockSpec((n_tok, D_MODEL), lambda b, p, pt: (0, 0)),
                  tok_spec,
                  pl.BlockSpec((1, LANE), lambda b, p, pt: (0, 0)), vec, vec, vec, vec]
        + [page_spec(r) for r in range(pages)] + [page_spec(r) for r in range(pages)],
        out_specs=tok_spec,
        scratch_shapes=[
            pltpu.VMEM((ATT_HEADS, n_rows, LANE), BF16),
            pltpu.VMEM((ATT_HEADS, pages * PAGE_SIZE, LANE), BF16),
            pltpu.VMEM((ATT_HEADS, n_rows, LANE), F32),
            pltpu.VMEM((ATT_HEADS, n_rows, LANE), F32),
            pltpu.VMEM((ATT_HEADS, n_rows, LANE), F32),
            pltpu.VMEM((ATT_HEADS, n_rows, LANE), F32),
        ],
    )
    return pl.pallas_call(
        kern,
        grid_spec=grid_spec,
        out_shape=jax.ShapeDtypeStruct((n_batch, t_new, D_MODEL), F32),
        compiler_params=_params(("arbitrary", "arbitrary")),
        name="attn_sample",
    )(page_table, q, knt, v_new, gate, subln_g.reshape(1, LANE),
      *[t.reshape(1, ATT_HEAD_DIM) for t in lams], *([cache_kt] * pages), *([cache_v2] * pages))


def _conv_in_kernel(*refs, nb, tb, tiles_per_batch, has_state):
    if has_state:
        (x_ref, ng_ref, wa_ref, wl_ref, wg_ref, cw_ref, cb_ref, st_in_ref,
         y_ref, gate_ref, st_ref, h_s, ext_s, wb_s) = refs
        carry_s = None
    else:
        (x_ref, ng_ref, wa_ref, wl_ref, wg_ref, cw_ref, cb_ref,
         y_ref, gate_ref, st_ref, h_s, ext_s, wb_s, carry_s) = refs
    i = pl.program_id(0)
    cb = pl.program_id(1)

    @pl.when(cb == 0)
    def _():
        _rms_rows(x_ref, ng_ref, h_s)

    a = jnp.dot(h_s[...], wa_ref[...], preferred_element_type=F32)
    glu = jnp.dot(h_s[...], wl_ref[...], preferred_element_type=F32)
    c = a * jax.nn.sigmoid(glu)

    head0 = CONV_HEAD - CONV_STATE
    if has_state:
        ext_s[:, 0:head0, :] = jnp.zeros((nb, head0, COL_BLOCK), F32)
        ext_s[:, head0:CONV_HEAD, :] = st_in_ref[...]
    else:
        first = (i % tiles_per_batch) == 0

        @pl.when(first)
        def _():
            ext_s[0, 0:CONV_HEAD, :] = jnp.zeros((CONV_HEAD, COL_BLOCK), F32)

        @pl.when(jnp.logical_not(first))
        def _():
            ext_s[0, 0:CONV_HEAD, :] = carry_s[cb]

    for b in range(nb):
        ext_s[b, CONV_HEAD:CONV_HEAD + tb, :] = c[b * tb:(b + 1) * tb, :]

    rc = min(tb, ROW_CHUNK)
    n_lane = COL_BLOCK // LANE_CHUNK
    n_row = tb // rc

    for j in range(CONV_WIDTH):
        wb_s[j] = jnp.broadcast_to(cw_ref[j:j + 1, :], (SUBLANES, COL_BLOCK))

    def chunk(it, carry):
        b = it // (n_row * n_lane)
        r0 = pl.multiple_of(((it // n_lane) % n_row) * rc, rc)
        l0 = pl.multiple_of((it % n_lane) * LANE_CHUNK, LANE_CHUNK)
        blk = ext_s[b, pl.ds(r0, rc + CONV_HEAD), pl.ds(l0, LANE_CHUNK)]
        acc = jnp.broadcast_to(cb_ref[:, pl.ds(l0, LANE_CHUNK)], (rc, LANE_CHUNK))
        for r in range(SUBLANES):
            shifted = blk if r == 0 else blk[r:r + rc + CONV_HEAD - SUBLANES, :]
            for j in range(CONV_WIDTH):
                if (head0 + j) % SUBLANES == r:
                    a = head0 + j - r
                    w_j = pltpu.repeat(wb_s[j, :, pl.ds(l0, LANE_CHUNK)], rc // SUBLANES, axis=0)
                    acc = acc + w_j * shifted[a:a + rc, :]
        y_ref[pl.ds(b * tb + r0, rc), pl.ds(l0, LANE_CHUNK)] = acc
        return carry

    lax.fori_loop(0, nb * n_row * n_lane, chunk, 0, unroll=2)
    col0 = pl.multiple_of(cb * COL_BLOCK, COL_BLOCK)
    if has_state:
        for b in range(nb):
            st_ref[b, :, pl.ds(col0, COL_BLOCK)] = ext_s[b, tb + head0:tb + CONV_HEAD, :]
    else:
        @pl.when((i % tiles_per_batch) == tiles_per_batch - 1)
        def _():
            st_ref[i // tiles_per_batch, :, pl.ds(col0, COL_BLOCK)] = ext_s[0, tb + head0:tb + CONV_HEAD, :]

        carry_s[cb] = ext_s[0, tb:tb + CONV_HEAD, :]

    gate_ref[...] = jnp.dot(h_s[...], wg_ref[...], preferred_element_type=F32).astype(BF16)


def _conv_in(x2d, norm_g, w_in, conv_w, conv_b, state, *, n_batch, seq, tm):
    m = x2d.shape[0]
    has_state = state is not None
    if has_state:
        assert tm == m
        nb, tb, tiles_per_batch = n_batch, seq, 1
    else:
        assert seq % tm == 0 and tm >= CONV_HEAD
        nb, tb, tiles_per_batch = 1, tm, seq // tm
    n_cb = D_INNER // COL_BLOCK
    in_specs = [
        pl.BlockSpec((tm, D_MODEL), lambda i, c: (i, 0)),
        pl.BlockSpec((1, D_MODEL), lambda i, c: (0, 0)),
        pl.BlockSpec((D_MODEL, COL_BLOCK), lambda i, c: (0, c)),
        pl.BlockSpec((D_MODEL, COL_BLOCK), lambda i, c: (0, n_cb + c)),
        pl.BlockSpec((D_MODEL, COL_BLOCK), lambda i, c: (0, 2 * n_cb + c)),
        pl.BlockSpec((CONV_WIDTH, COL_BLOCK), lambda i, c: (0, c)),
        pl.BlockSpec((1, COL_BLOCK), lambda i, c: (0, c)),
    ]
    args = [x2d, norm_g.reshape(1, D_MODEL), w_in, w_in, w_in, conv_w, conv_b.reshape(1, D_INNER)]
    scratch = [
        pltpu.VMEM((tm, D_MODEL), BF16),
        pltpu.VMEM((nb, CONV_HEAD + tb, COL_BLOCK), F32),
        pltpu.VMEM((CONV_WIDTH, SUBLANES, COL_BLOCK), F32),
    ]
    if has_state:
        in_specs.append(pl.BlockSpec((nb, CONV_STATE, COL_BLOCK), lambda i, c: (0, 0, c)))
        args.append(state)
    else:
        scratch.append(pltpu.VMEM((n_cb, CONV_HEAD, COL_BLOCK), F32))
    kern = functools.partial(_conv_in_kernel, nb=nb, tb=tb, tiles_per_batch=tiles_per_batch,
                             has_state=has_state)
    return pl.pallas_call(
        kern,
        grid=(m // tm, n_cb),
        in_specs=in_specs,
        out_specs=[
            pl.BlockSpec((tm, COL_BLOCK), lambda i, c: (i, c)),
            pl.BlockSpec((tm, COL_BLOCK), lambda i, c: (i, c)),
            pl.BlockSpec((n_batch, CONV_STATE, D_INNER), lambda i, c: (0, 0, 0)),
        ],
        out_shape=[
            jax.ShapeDtypeStruct((m, D_INNER), F32),
            jax.ShapeDtypeStruct((m, D_INNER), BF16),
            jax.ShapeDtypeStruct((n_batch, CONV_STATE, D_INNER), F32),
        ],
        scratch_shapes=scratch,
        compiler_params=_params(("arbitrary", "arbitrary")),
        name="conv_in",
    )(*args)


PROMPT_TM = 512
OUT_TM = 1024
LN_TM = 512
LN_TN = 512
ATTN_TQ = 1024
ATTN_TK = 1024
SAMPLE_PAGES_PER_STEP = 8


def kernel(x_prompt, x_sample, state_pool_l0, cache_k_l1, cache_v_l1, state_conv_l2, state_pool_l3, page_table, norm_g_l0, w_in_l0, w_grp_l0, pool_scale_l0, w_out_l0, norm_g_l1, w_in_l1, q_norm_g_l1, k_norm_g_l1, lambda_q1_l1, lambda_k1_l1, lambda_q2_l1, lambda_k2_l1, subln_g_l1, w_out_l1, norm_g_l2, w_in_l2, conv_w_l2, conv_b_l2, ln_g_l2, ln_b_l2, w_out_l2, norm_g_l3, w_in_l3, w_grp_l3, pool_scale_l3, w_out_l3):
    nbp, seq, _ = x_prompt.shape
    nbs, t_new, _ = x_sample.shape
    mp, ms = nbp * seq, nbs * t_new
    xp = x_prompt.reshape(mp, D_MODEL)
    xs = x_sample.reshape(ms, D_MODEL)
    lams = (lambda_q1_l1, lambda_k1_l1, lambda_q2_l1, lambda_k2_l1)

    def bf(w):
        return w.astype(BF16)

    def pool_layer(xp, xs, state, norm_g, w_in, w_grp, scale, w_out):
        w_in, w_grp, w_out = bf(w_in), bf(w_grp), bf(w_out)
        zp, stp = _pool_in(xp, norm_g, w_in, w_grp, scale, None, n_batch=nbp, seq=seq, tm=PROMPT_TM)
        zs, sts = _pool_in(xs, norm_g, w_in, w_grp, scale, state, n_batch=nbs, seq=t_new, tm=ms)
        xp = _out_proj(zp, w_out, xp, tm=OUT_TM, tn=COL_BLOCK)
        xs = _out_proj(zs, w_out, xs, tm=ms, tn=COL_BLOCK)
        return xp, xs, stp, sts

    xp, xs, pool0_p, pool0_s = pool_layer(xp, xs, state_pool_l0, norm_g_l0, w_in_l0, w_grp_l0,
                                          pool_scale_l0, w_out_l0)

    w_in, w_out = bf(w_in_l1), bf(w_out_l1)
    w_kt = bf(lax.slice(w_in_l1, (0, D_MODEL), (D_MODEL, 2 * D_MODEL))).T
    qp, ktp, vp, vnp, gp = _attn_in(xp, norm_g_l1, w_in, w_kt, q_norm_g_l1, k_norm_g_l1, n_batch=nbp, seq=seq,
                                    tm=PROMPT_TM, q_dtype=BF16, v_dtype=BF16, native_v=True)
    qs, kts, vs, gs = _attn_in(xs, norm_g_l1, w_in, w_kt, q_norm_g_l1, k_norm_g_l1, n_batch=1, seq=ms,
                               tm=ms, q_dtype=F32, v_dtype=F32, native_v=False)
    shp = (nbp, seq, D_MODEL)
    zp = _attn_prompt(qp.reshape(shp), ktp, vp.reshape(shp), gp.reshape(shp), subln_g_l1, lams,
                      n_batch=nbp, seq=seq, tq=ATTN_TQ, tk=ATTN_TK)
    n_phys = cache_k_l1.shape[0]
    cache_kt = jnp.transpose(cache_k_l1, (0, 2, 3, 4, 1)).reshape(n_phys, D_MODEL, PAGE_SIZE)
    cache_v2 = cache_v_l1.reshape(n_phys, PAGE_SIZE * ATT_HEADS, ATT_VDIM)
    shs = (nbs, t_new, D_MODEL)
    zs = _attn_sample(qs.reshape(shs), kts[0], vs, gs.reshape(shs), cache_kt, cache_v2, page_table,
                      subln_g_l1, lams, pages=SAMPLE_PAGES_PER_STEP)
    xp = _out_proj(zp.reshape(mp, D_MODEL), w_out, xp, tm=OUT_TM, tn=COL_BLOCK)
    xs = _out_proj(zs.reshape(ms, D_MODEL).astype(BF16), w_out, xs, tm=ms, tn=COL_BLOCK)
    new_k_p = jnp.transpose(ktp.reshape(nbp, ATT_HEADS, 2, ATT_HEAD_DIM, seq), (0, 4, 1, 2, 3))
    new_v_p = vnp.reshape(nbp, seq, ATT_HEADS, ATT_VDIM)
    new_k_s = kts[0].T.reshape(nbs, t_new, ATT_HEADS, 2, ATT_HEAD_DIM)
    new_v_s = vs.reshape(nbs, t_new, ATT_HEADS, ATT_VDIM)

    w_in, w_out = bf(w_in_l2), bf(w_out_l2)
    yp, gp, conv_p = _conv_in(xp, norm_g_l2, w_in, conv_w_l2, conv_b_l2, None, n_batch=nbp, seq=seq,
                              tm=PROMPT_TM)
    ys, gs, conv_s = _conv_in(xs, norm_g_l2, w_in, conv_w_l2, conv_b_l2, state_conv_l2, n_batch=nbs,
                              seq=t_new, tm=ms)
    xp = _ln_out_proj(yp, gp, ln_g_l2, ln_b_l2, w_out, xp, tm=LN_TM, tn=LN_TN)
    xs = _ln_out_proj(ys, gs, ln_g_l2, ln_b_l2, w_out, xs, tm=ms, tn=LN_TN)

    xp, xs, pool3_p, pool3_s = pool_layer(xp, xs, state_pool_l3, norm_g_l3, w_in_l3, w_grp_l3,
                                          pool_scale_l3, w_out_l3)

    return (xp.reshape(nbp, seq, D_MODEL), xs.reshape(nbs, t_new, D_MODEL), pool0_p, pool0_s,
            new_k_p, new_v_p, new_k_s, new_v_s, conv_p, conv_s, pool3_p, pool3_s)
```

```python
import functools
import math

import jax
import jax.numpy as jnp
from jax import lax
from jax.experimental import pallas as pl
from jax.experimental.pallas import tpu as pltpu

F32 = jnp.float32
BF16 = jnp.bfloat16

D_MODEL = 2048
D_INNER = 4096
POOL_WINDOWS = (2, 4, 8, 16)
POOL_GROUP = 1024
POOL_STATE = 15
ATT_HEADS = 16
ATT_HEAD_DIM = 64
ATT_VDIM = 128
LAMBDA_INIT = 0.8 - 0.6 * math.exp(-0.3 * 1)
LOG2E = math.log2(math.e)
QK_SCALE = ATT_HEAD_DIM ** -0.5 * LOG2E
CONV_WIDTH = 31
CONV_STATE = 30
NORM_EPS = 1e-6
NEG_INF = -1e30
PAGE_SIZE = 128

VMEM_LIMIT_BYTES = 58 * 1024 * 1024
LANE = 128
SUBLANES = 8
BF16_ROWS = 16
COL_BLOCK = 1024
POOL_HEAD = 16
CONV_HEAD = 32
ROW_CHUNK = 32
LANE_CHUNK = 256


def _params(sem):
    return pltpu.CompilerParams(dimension_semantics=sem, vmem_limit_bytes=VMEM_LIMIT_BYTES)


def _rms_rows(x_ref, g_ref, h_ref):
    xf = x_ref[...]
    ms = jnp.mean(xf * xf, axis=-1, keepdims=True)
    h_ref[...] = (xf * lax.rsqrt(ms + NORM_EPS) * g_ref[...]).astype(BF16)


def _silu(x):
    return x * jax.nn.sigmoid(x)


def _pool_rows(blk, w, n_seen):
    rows = blk.shape[0] - POOL_HEAD
    cur = blk[POOL_HEAD:, :]
    acc = cur
    for j in range(1, w):
        acc = acc + blk[POOL_HEAD - j:POOL_HEAD - j + rows, :]
    cnt = jnp.minimum(n_seen, w).astype(F32)
    return jnp.where(n_seen >= w, acc * (1.0 / w), acc / cnt) - cur


def _pool_finish(h_s, pooled, wg_ref, wgrp_ref, sc_ref, z_ref):
    mixed = jnp.dot(pooled, wgrp_ref[0], preferred_element_type=F32) * sc_ref[...]
    gate = jnp.dot(h_s[...], wg_ref[...], preferred_element_type=F32)
    z_ref[...] = (mixed * _silu(gate)).astype(BF16)


def _pool_in_prompt_kernel(x_ref, ng_ref, wu_ref, wg_ref, wgrp_ref, sc_ref, pmat_ref,
                           z_ref, st_ref, h_s, edge_s, pooled_s, carry_s, *, tm, tiles_per_batch):
    i = pl.program_id(0)
    g = pl.program_id(1)

    @pl.when(g == 0)
    def _():
        _rms_rows(x_ref, ng_ref, h_s)

    u = jnp.dot(h_s[...], wu_ref[...], preferred_element_type=F32)
    first = (i % tiles_per_batch) == 0

    pooled_s[...] = jnp.dot(pmat_ref[0], u.astype(BF16), preferred_element_type=F32).astype(BF16)

    @pl.when(first)
    def _():
        edge_s[0:POOL_HEAD, :] = jnp.zeros((POOL_HEAD, COL_BLOCK), F32)

    @pl.when(jnp.logical_not(first))
    def _():
        edge_s[0:POOL_HEAD, :] = carry_s[g]

    edge_s[POOL_HEAD:, :] = u[0:POOL_HEAD, :]
    n_seen = ((i % tiles_per_batch) * tm + 1
              + lax.broadcasted_iota(jnp.int32, (POOL_HEAD, LANE_CHUNK), 0))
    for k, w in enumerate(POOL_WINDOWS):
        @pl.when(g == k)
        def _(w=w):
            for l0 in range(0, COL_BLOCK, LANE_CHUNK):
                pooled_s[0:POOL_HEAD, l0:l0 + LANE_CHUNK] = _pool_rows(
                    edge_s[:, l0:l0 + LANE_CHUNK], w, n_seen).astype(BF16)

    @pl.when((i % tiles_per_batch) == tiles_per_batch - 1)
    def _():
        col0 = pl.multiple_of(g * COL_BLOCK, COL_BLOCK)
        st_ref[i // tiles_per_batch, :, pl.ds(col0, COL_BLOCK)] = u[tm - POOL_STATE:tm, :]

    carry_s[g] = u[tm - POOL_HEAD:tm, :]
    _pool_finish(h_s, pooled_s[...], wg_ref, wgrp_ref, sc_ref, z_ref)


def _pool_in_sample_kernel(x_ref, ng_ref, wu_ref, wg_ref, wgrp_ref, sc_ref, st_in_ref,
                           z_ref, st_ref, h_s, ext_s, pooled_s, *, nb, tb):
    g = pl.program_id(1)

    @pl.when(g == 0)
    def _():
        _rms_rows(x_ref, ng_ref, h_s)

    u = jnp.dot(h_s[...], wu_ref[...], preferred_element_type=F32)

    ext_s[:, 0:1, :] = jnp.zeros((nb, 1, COL_BLOCK), F32)
    ext_s[:, 1:POOL_HEAD, :] = st_in_ref[...]
    for b in range(nb):
        ext_s[b, POOL_HEAD:, :] = u[b * tb:(b + 1) * tb, :]

    n_seen = POOL_STATE + 1 + lax.broadcasted_iota(jnp.int32, (tb, LANE_CHUNK), 0)
    for k, w in enumerate(POOL_WINDOWS):
        @pl.when(g == k)
        def _(w=w):
            for b in range(nb):
                for l0 in range(0, COL_BLOCK, LANE_CHUNK):
                    pooled_s[b * tb:(b + 1) * tb, l0:l0 + LANE_CHUNK] = _pool_rows(
                        ext_s[b, :, l0:l0 + LANE_CHUNK], w, n_seen)

    col0 = pl.multiple_of(g * COL_BLOCK, COL_BLOCK)
    for b in range(nb):
        st_ref[b, :, pl.ds(col0, COL_BLOCK)] = ext_s[b, tb + 1:tb + POOL_HEAD, :]
    _pool_finish(h_s, pooled_s[...].astype(BF16), wg_ref, wgrp_ref, sc_ref, z_ref)


def _pool_matrices(tm):
    t = jnp.arange(tm)[:, None]
    j = jnp.arange(tm)[None, :]
    mats = [jnp.where((t - j >= 0) & (t - j < w), 1.0 / w, 0.0) - (t == j) for w in POOL_WINDOWS]
    return jnp.stack(mats).astype(BF16)


def _pool_in(x2d, norm_g, w_in, w_grp, scale, state, *, n_batch, seq, tm):
    m = x2d.shape[0]
    n_groups = len(POOL_WINDOWS)
    in_specs = [
        pl.BlockSpec((tm, D_MODEL), lambda i, g: (i, 0)),
        pl.BlockSpec((1, D_MODEL), lambda i, g: (0, 0)),
        pl.BlockSpec((D_MODEL, COL_BLOCK), lambda i, g: (0, g)),
        pl.BlockSpec((D_MODEL, COL_BLOCK), lambda i, g: (0, n_groups + g)),
        pl.BlockSpec((1, POOL_GROUP, POOL_GROUP), lambda i, g: (g, 0, 0)),
        pl.BlockSpec((1, COL_BLOCK), lambda i, g: (0, g)),
    ]
    args = [x2d, norm_g.reshape(1, D_MODEL), w_in, w_in, w_grp, scale.reshape(1, D_INNER)]
    if state is None:
        assert seq % tm == 0 and tm >= POOL_HEAD
        in_specs.append(pl.BlockSpec((1, tm, tm), lambda i, g: (g, 0, 0)))
        args.append(_pool_matrices(tm))
        scratch = [
            pltpu.VMEM((tm, D_MODEL), BF16),
            pltpu.VMEM((2 * POOL_HEAD, COL_BLOCK), F32),
            pltpu.VMEM((tm, COL_BLOCK), BF16),
            pltpu.VMEM((n_groups, POOL_HEAD, COL_BLOCK), F32),
        ]
        kern = functools.partial(_pool_in_prompt_kernel, tm=tm, tiles_per_batch=seq // tm)
    else:
        assert tm == m
        in_specs.append(pl.BlockSpec((n_batch, POOL_STATE, COL_BLOCK), lambda i, g: (0, 0, g)))
        args.append(state)
        scratch = [
            pltpu.VMEM((tm, D_MODEL), BF16),
            pltpu.VMEM((n_batch, POOL_HEAD + seq, COL_BLOCK), F32),
            pltpu.VMEM((tm, COL_BLOCK), F32),
        ]
        kern = functools.partial(_pool_in_sample_kernel, nb=n_batch, tb=seq)
    return pl.pallas_call(
        kern,
        grid=(m // tm, n_groups),
        in_specs=in_specs,
        out_specs=[
            pl.BlockSpec((tm, COL_BLOCK), lambda i, g: (i, g)),
            pl.BlockSpec((n_batch, POOL_STATE, D_INNER), lambda i, g: (0, 0, 0)),
        ],
        out_shape=[
            jax.ShapeDtypeStruct((m, D_INNER), BF16),
            jax.ShapeDtypeStruct((n_batch, POOL_STATE, D_INNER), F32),
        ],
        scratch_shapes=scratch,
        compiler_params=_params(("arbitrary", "arbitrary")),
        name="pool_in",
    )(*args)


def _out_proj_kernel(z_ref, w_ref, x_ref, o_ref):
    o_ref[...] = x_ref[...] + jnp.dot(z_ref[...], w_ref[...], preferred_element_type=F32)


def _out_proj(z, w, x2d, *, tm, tn):
    m, k = z.shape
    n = w.shape[1]
    return pl.pallas_call(
        _out_proj_kernel,
        grid=(m // tm, n // tn),
        in_specs=[
            pl.BlockSpec((tm, k), lambda i, j: (i, 0)),
            pl.BlockSpec((k, tn), lambda i, j: (0, j)),
            pl.BlockSpec((tm, tn), lambda i, j: (i, j)),
        ],
        out_specs=pl.BlockSpec((tm, tn), lambda i, j: (i, j)),
        out_shape=jax.ShapeDtypeStruct((m, n), F32),
        compiler_params=_params(("arbitrary", "arbitrary")),
        name="out_proj",
    )(z, w, x2d)


def _ln_out_proj_kernel(y_ref, gate_ref, lg_ref, lb_ref, w_ref, x_ref, o_ref, z_s):
    @pl.when(pl.program_id(1) == 0)
    def _():
        def rows(it, carry):
            r0 = pl.multiple_of(it * BF16_ROWS, BF16_ROWS)
            y = y_ref[pl.ds(r0, BF16_ROWS), :]
            yc = y - jnp.mean(y, axis=-1, keepdims=True)
            var = jnp.mean(yc * yc, axis=-1, keepdims=True)
            ln = yc * lax.rsqrt(var + NORM_EPS) * lg_ref[...] + lb_ref[...]
            gate = gate_ref[pl.ds(r0, BF16_ROWS), :].astype(F32)
            z_s[pl.ds(r0, BF16_ROWS), :] = (_silu(ln) * _silu(gate)).astype(BF16)
            return carry

        lax.fori_loop(0, z_s.shape[0] // BF16_ROWS, rows, 0, unroll=4)

    o_ref[...] = x_ref[...] + jnp.dot(z_s[...], w_ref[...], preferred_element_type=F32)


def _ln_out_proj(y, gate, ln_g, ln_b, w, x2d, *, tm, tn):
    m, k = y.shape
    n = w.shape[1]
    return pl.pallas_call(
        _ln_out_proj_kernel,
        grid=(m // tm, n // tn),
        in_specs=[
            pl.BlockSpec((tm, k), lambda i, j: (i, 0)),
            pl.BlockSpec((tm, k), lambda i, j: (i, 0)),
            pl.BlockSpec((1, k), lambda i, j: (0, 0)),
            pl.BlockSpec((1, k), lambda i, j: (0, 0)),
            pl.BlockSpec((k, tn), lambda i, j: (0, j)),
            pl.BlockSpec((tm, tn), lambda i, j: (i, j)),
        ],
        out_specs=pl.BlockSpec((tm, tn), lambda i, j: (i, j)),
        out_shape=jax.ShapeDtypeStruct((m, n), F32),
        scratch_shapes=[pltpu.VMEM((tm, k), BF16)],
        compiler_params=_params(("arbitrary", "arbitrary")),
        name="ln_out_proj",
    )(y, gate, ln_g.reshape(1, k), ln_b.reshape(1, k), w, x2d)


def _head_norm(y, g128):
    lane = lax.broadcasted_iota(jnp.int32, (1, LANE), 1)
    low = lane < ATT_HEAD_DIM
    outs = []
    for c in range(y.shape[1] // LANE):
        blk = y[:, c * LANE:(c + 1) * LANE]
        sq = blk * blk
        s_lo = jnp.sum(jnp.where(low, sq, 0.0), axis=-1, keepdims=True)
        s_hi = jnp.sum(jnp.where(low, 0.0, sq), axis=-1, keepdims=True)
        ms = jnp.where(low, s_lo, s_hi) * (1.0 / ATT_HEAD_DIM)
        outs.append(blk * lax.rsqrt(ms + NORM_EPS) * g128)
    return jnp.concatenate(outs, axis=1)


def _attn_in_kernel(x_ref, ng_ref, w_ref, qg_ref, kg_ref, q_ref, kt_ref, v_ref, vn_ref, gate_ref, h_s,
                    *, q_dtype, v_dtype, tm):
    j = pl.program_id(1)

    @pl.when(j == 0)
    def _():
        _rms_rows(x_ref, ng_ref, h_s)

    @pl.when(j < 2)
    def _():
        y = jnp.dot(h_s[...], w_ref[...], preferred_element_type=F32)
        q_ref[...] = (_head_norm(y, qg_ref[...]) * QK_SCALE).astype(q_dtype)

    @pl.when(jnp.logical_and(j >= 2, j < 4))
    def _():
        yt = lax.dot_general(w_ref[...], h_s[...], (((0,), (1,)), ((), ())), preferred_element_type=F32)
        y3 = yt.reshape(COL_BLOCK // ATT_HEAD_DIM, ATT_HEAD_DIM, tm)
        ms = jnp.mean(y3 * y3, axis=1, keepdims=True)
        kt_ref[...] = (y3 * lax.rsqrt(ms + NORM_EPS) * kg_ref[...][None]).reshape(COL_BLOCK, tm)

    for jj in (4, 5):
        @pl.when(j == jj)
        def _(jj=jj):
            y = jnp.dot(h_s[...], w_ref[...], preferred_element_type=F32)
            v_ref[...] = y.astype(v_dtype)
            if vn_ref is not None:
                for hl in range(COL_BLOCK // LANE):
                    hh = (jj - 4) * (COL_BLOCK // LANE) + hl
                    vn_ref[pl.ds(hh, tm, stride=ATT_HEADS), :] = y[:, hl * LANE:(hl + 1) * LANE]

    @pl.when(j >= 6)
    def _():
        gate_ref[...] = jnp.dot(h_s[...], w_ref[...], preferred_element_type=F32)


def _attn_in(x2d, norm_g, w_in, q_norm_g, k_norm_g, *, n_batch, seq, tm, q_dtype, v_dtype, native_v):
    m = x2d.shape[0]
    tiles_per_batch = seq // tm
    n_col = w_in.shape[1] // COL_BLOCK

    def out_spec(first):
        return pl.BlockSpec((tm, COL_BLOCK), lambda i, j: (i, jnp.clip(j - first, 0, 1)))

    out_specs = [
        out_spec(0),
        pl.BlockSpec((None, COL_BLOCK, tm),
                     lambda i, j: (i // tiles_per_batch, jnp.clip(j - 2, 0, 1), i % tiles_per_batch)),
        out_spec(4),
    ]
    out_shape = [
        jax.ShapeDtypeStruct((m, D_MODEL), q_dtype),
        jax.ShapeDtypeStruct((n_batch, D_MODEL, seq), F32),
        jax.ShapeDtypeStruct((m, D_MODEL), v_dtype),
    ]
    if native_v:
        out_specs.append(pl.BlockSpec((tm * ATT_HEADS, LANE), lambda i, j: (i, 0)))
        out_shape.append(jax.ShapeDtypeStruct((m * ATT_HEADS, LANE), F32))
    out_specs.append(out_spec(6))
    out_shape.append(jax.ShapeDtypeStruct((m, D_MODEL), F32))

    def kern(*refs):
        ins, outs, scratch = refs[:5], refs[5:-1], refs[-1]
        if native_v:
            q_ref, kt_ref, v_ref, vn_ref, gate_ref = outs
        else:
            (q_ref, kt_ref, v_ref, gate_ref), vn_ref = outs, None
        _attn_in_kernel(*ins, q_ref, kt_ref, v_ref, vn_ref, gate_ref, scratch,
                        q_dtype=q_dtype, v_dtype=v_dtype, tm=tm)

    return pl.pallas_call(
        kern,
        grid=(m // tm, n_col),
        in_specs=[
            pl.BlockSpec((tm, D_MODEL), lambda i, j: (i, 0)),
            pl.BlockSpec((1, D_MODEL), lambda i, j: (0, 0)),
            pl.BlockSpec((D_MODEL, COL_BLOCK), lambda i, j: (0, j)),
            pl.BlockSpec((1, LANE), lambda i, j: (0, 0)),
            pl.BlockSpec((ATT_HEAD_DIM, 1), lambda i, j: (0, 0)),
        ],
        out_specs=out_specs,
        out_shape=out_shape,
        scratch_shapes=[pltpu.VMEM((tm, D_MODEL), BF16)],
        compiler_params=_params(("arbitrary", "arbitrary")),
        name="attn_in",
    )(x2d, norm_g.reshape(1, D_MODEL), w_in, jnp.tile(q_norm_g, 2).reshape(1, LANE),
      k_norm_g.reshape(ATT_HEAD_DIM, 1))


def _lambda(lq1_ref, lk1_ref, lq2_ref, lk2_ref):
    a = jnp.sum(lq1_ref[...] * lk1_ref[...], axis=-1, keepdims=True)
    b = jnp.sum(lq2_ref[...] * lk2_ref[...], axis=-1, keepdims=True)
    return jnp.exp(a) - jnp.exp(b) + LAMBDA_INIT


def _sub_ln_gate(o, subln_g, gate):
    ms = jnp.mean(o * o, axis=-1, keepdims=True)
    o = o * lax.rsqrt(ms + NORM_EPS) * subln_g * (1.0 - LAMBDA_INIT)
    return o * _silu(gate)


def _attn_prompt_kernel(q_ref, kt_ref, v_ref, gate_ref, sg_ref, lq1_ref, lk1_ref, lq2_ref, lk2_ref,
                        z_ref, ka_s, m_s, l_s, acc_s, *, tq, tk):
    h = pl.program_id(1)
    qi = pl.program_id(2)
    seq = kt_ref.shape[1]
    slope2 = jnp.exp2(-0.5 * jnp.full((1, LANE), h + 1, jnp.int32).astype(F32)) * LOG2E

    @pl.when(qi == 0)
    def _():
        col = lax.broadcasted_iota(jnp.int32, (BF16_ROWS, seq), 1) % tk
        row = lax.broadcasted_iota(jnp.int32, (BF16_ROWS, seq), 0)
        b = pltpu.repeat(slope2, seq // LANE, axis=1) * col.astype(F32)
        hi = b.astype(BF16).astype(F32)
        mid = (b - hi).astype(BF16).astype(F32)
        lo = b - hi - mid
        rows = jnp.where(row == 0, hi, jnp.where(row == 1, mid, jnp.where(row == 2, lo, 0.0))).astype(BF16)
        zeros = jnp.zeros((ATT_HEAD_DIM - BF16_ROWS, seq), BF16)
        ka_s[0, 0:ATT_HEAD_DIM, :] = kt_ref[0:ATT_HEAD_DIM, :].astype(BF16)
        ka_s[0, ATT_HEAD_DIM:ATT_HEAD_DIM + BF16_ROWS, :] = rows
        ka_s[0, ATT_HEAD_DIM + BF16_ROWS:, :] = zeros
        ka_s[1, 0:BF16_ROWS, :] = rows
        ka_s[1, BF16_ROWS:ATT_HEAD_DIM, :] = zeros
        ka_s[1, ATT_HEAD_DIM:, :] = kt_ref[ATT_HEAD_DIM:, :].astype(BF16)

    q = q_ref[...]
    lane = lax.broadcasted_iota(jnp.int32, (1, LANE), 1)
    low = lane < ATT_HEAD_DIM
    zero = jnp.zeros_like(q)
    ones1 = jnp.where(jnp.logical_and(lane >= ATT_HEAD_DIM, lane < ATT_HEAD_DIM + 3), 1.0, 0.0).astype(BF16)
    ones2 = jnp.where(lane < 3, 1.0, 0.0).astype(BF16)
    qa = (jnp.where(low, q, zero) + ones1, jnp.where(low, zero, q) + ones2)

    m_s[...] = jnp.full(m_s.shape, NEG_INF, F32)
    l_s[...] = jnp.zeros(l_s.shape, F32)
    acc_s[...] = jnp.zeros(acc_s.shape, F32)

    def tile(kj, masked, r0=0, r1=tq, n_col=tk):
        start = pl.multiple_of(kj * tk, tk)
        vt = v_ref[pl.ds(start, n_col), :]
        shift = slope2 * jnp.full((1, LANE), kj * tk - qi * tq, jnp.int32).astype(F32)
        for c in range(2):
            s = jnp.dot(qa[c][r0:r1], ka_s[c, :, pl.ds(start, n_col)], preferred_element_type=F32)
            if masked:
                row_i = lax.broadcasted_iota(jnp.int32, (r1 - r0, n_col), 0) + r0
                col_j = lax.broadcasted_iota(jnp.int32, (r1 - r0, n_col), 1) + (kj * tk - qi * tq)
                s = jnp.where(col_j <= row_i, s, NEG_INF)
            m_old = m_s[c, r0:r1]
            m_new = jnp.maximum(m_old, jnp.max(s, axis=-1, keepdims=True) + shift)
            alpha = jnp.exp2(m_old - m_new)
            p = jnp.exp2(s - pltpu.repeat(m_new - shift, n_col // LANE, axis=1))
            l_s[c, r0:r1] = alpha * l_s[c, r0:r1] + jnp.sum(p, axis=-1, keepdims=True)
            acc_s[c, r0:r1] = alpha * acc_s[c, r0:r1] + jnp.dot(p.astype(BF16), vt, preferred_element_type=F32)
            m_s[c, r0:r1] = m_new

    n_full = (qi * tq) // tk

    def body(kj, carry):
        tile(kj, False)
        return carry

    lax.fori_loop(0, n_full, body, 0)
    if tq == tk:
        tile(n_full, True, 0, tq // 2, tk // 2)
        tile(n_full, True, tq // 2, tq, tk)
    else:
        for d in range(tq // tk):
            tile(n_full + d, True)

    lam = _lambda(lq1_ref, lk1_ref, lq2_ref, lk2_ref)
    o = acc_s[0] / l_s[0] - lam * (acc_s[1] / l_s[1])
    z_ref[...] = _sub_ln_gate(o, sg_ref[...], gate_ref[...]).astype(BF16)


def _attn_prompt(q, kt, v, gate, subln_g, lams, *, n_batch, seq, tq, tk):
    assert tq % tk == 0 and seq % tq == 0
    vec = pl.BlockSpec((1, ATT_HEAD_DIM), lambda b, h, i: (0, 0))
    kern = functools.partial(_attn_prompt_kernel, tq=tq, tk=tk)
    return pl.pallas_call(
        kern,
        grid=(n_batch, ATT_HEADS, seq // tq),
        in_specs=[
            pl.BlockSpec((None, tq, LANE), lambda b, h, i: (b, i, h)),
            pl.BlockSpec((None, LANE, seq), lambda b, h, i: (b, h, 0)),
            pl.BlockSpec((None, seq, LANE), lambda b, h, i: (b, 0, h)),
            pl.BlockSpec((None, tq, LANE), lambda b, h, i: (b, i, h)),
            pl.BlockSpec((1, LANE), lambda b, h, i: (0, 0)),
            vec, vec, vec, vec,
        ],
        out_specs=pl.BlockSpec((None, tq, LANE), lambda b, h, i: (b, i, h)),
        out_shape=jax.ShapeDtypeStruct((n_batch, seq, D_MODEL), BF16),
        scratch_shapes=[
            pltpu.VMEM((2, LANE, seq), BF16),
            pltpu.VMEM((2, tq, LANE), F32),
            pltpu.VMEM((2, tq, LANE), F32),
            pltpu.VMEM((2, tq, LANE), F32),
        ],
        compiler_params=_params(("arbitrary", "arbitrary", "arbitrary")),
        name="attn_prompt",
    )(q, kt, v, gate, subln_g.reshape(1, LANE), *[t.reshape(1, ATT_HEAD_DIM) for t in lams])


def _attn_sample_kernel(pt_ref, q_ref, knt_ref, vn_ref, gate_ref, sg_ref,
                        lq1_ref, lk1_ref, lq2_ref, lk2_ref, *rest, pages, n_steps, past_len, t_new):
    k_refs = rest[:pages]
    v_refs = rest[pages:2 * pages]
    z_ref, q3_s, vh_s, slope_s, m_s, l_s, acc_s = rest[2 * pages:]
    b = pl.program_id(0)
    p = pl.program_id(1)
    n_rows = 2 * t_new
    shape3 = (ATT_HEADS, n_rows, LANE)

    @pl.when(p == 0)
    def _():
        row_c = lax.broadcasted_iota(jnp.int32, (n_rows, LANE), 0) // t_new
        lane_c = lax.broadcasted_iota(jnp.int32, (n_rows, LANE), 1) // ATT_HEAD_DIM
        for hh in range(ATT_HEADS):
            blk = q_ref[:, hh * LANE:(hh + 1) * LANE]
            both = jnp.concatenate([blk, blk], axis=0)
            q3_s[hh] = jnp.where(row_c == lane_c, both, 0.0).astype(BF16)
        head = lax.broadcasted_iota(jnp.int32, shape3, 0)
        slope_s[...] = jnp.exp2(-0.5 * (head + 1).astype(F32)) * LOG2E
        m_s[...] = jnp.full(shape3, NEG_INF, F32)
        l_s[...] = jnp.zeros(shape3, F32)
        acc_s[...] = jnp.zeros(shape3, F32)

    q3 = q3_s[...]

    def update(s, vb):
        m_old = m_s[...]
        m_new = jnp.maximum(m_old, jnp.max(s, axis=-1, keepdims=True))
        alpha = jnp.exp2(m_old - m_new)
        t = s.shape[-1]
        m_all = pltpu.repeat(m_new, t // LANE, axis=2) if t % LANE == 0 else m_new[:, :, 0:t]
        pr = jnp.exp2(s - m_all)
        l_s[...] = alpha * l_s[...] + jnp.sum(pr, axis=-1, keepdims=True)
        pv = jnp.einsum("hrt,hte->hre", pr.astype(BF16), vb, preferred_element_type=F32)
        acc_s[...] = alpha * acc_s[...] + pv
        m_s[...] = m_new

    tok = lax.broadcasted_iota(jnp.int32, (1, 1, PAGE_SIZE), 2)
    slope = slope_s[...]
    scores = []
    for r in range(pages):
        k3 = k_refs[r][...].reshape(ATT_HEADS, LANE, PAGE_SIZE).astype(BF16)
        k_pos = ((p * pages + r) * PAGE_SIZE - past_len + tok).astype(F32)
        scores.append(jnp.einsum("hrk,hkt->hrt", q3, k3, preferred_element_type=F32) + slope * k_pos)
        v3 = v_refs[r][...].reshape(PAGE_SIZE, ATT_HEADS, LANE)
        vh_s[:, r * PAGE_SIZE:(r + 1) * PAGE_SIZE, :] = jnp.swapaxes(v3, 0, 1).astype(BF16)
    update(jnp.concatenate(scores, axis=-1), vh_s[...])

    @pl.when(p == n_steps - 1)
    def _():
        n_tok = knt_ref.shape[1]
        k3 = knt_ref[...].reshape(ATT_HEADS, LANE, n_tok).astype(BF16)
        tok_n = lax.broadcasted_iota(jnp.int32, (1, n_rows, n_tok), 2)
        row_t = lax.broadcasted_iota(jnp.int32, (1, n_rows, n_tok), 1) % t_new
        valid = jnp.logical_and(tok_n // t_new == b, tok_n % t_new <= row_t)
        s = jnp.einsum("hrk,hkt->hrt", q3, k3, preferred_element_type=F32)
        s = s + slope[:, :, 0:n_tok] * (tok_n % t_new).astype(F32)
        s = jnp.where(valid, s, NEG_INF)
        vb = jnp.stack([vn_ref[:, hh * LANE:(hh + 1) * LANE] for hh in range(ATT_HEADS)], axis=0).astype(BF16)
        update(s, vb)

        lam = _lambda(lq1_ref, lk1_ref, lq2_ref, lk2_ref)
        acc = acc_s[...]
        l = l_s[...]
        o = acc[:, 0:t_new, :] / l[:, 0:t_new, :] - lam * (acc[:, t_new:, :] / l[:, t_new:, :])
        for hh in range(ATT_HEADS):
            gate = gate_ref[:, hh * LANE:(hh + 1) * LANE]
            z_ref[:, hh * LANE:(hh + 1) * LANE] = _sub_ln_gate(o[hh], sg_ref[...], gate)


def _attn_sample(q, knt, v_new, gate, cache_kt, cache_v2, page_table, subln_g, lams, *, pages):
    n_batch, t_new, _ = q.shape
    n_pages = page_table.shape[1]
    n_steps = n_pages // pages
    n_rows = 2 * t_new
    n_tok = n_batch * t_new

    tok_spec = pl.BlockSpec((None, t_new, D_MODEL), lambda b, p, pt: (b, 0, 0))
    vec = pl.BlockSpec((1, ATT_HEAD_DIM), lambda b, p, pt: (0, 0))

    def page_spec(r):
        return pl.BlockSpec((None, D_MODEL, PAGE_SIZE), lambda b, p, pt: (pt[b, p * pages + r], 0, 0))

    kern = functools.partial(_attn_sample_kernel, pages=pages, n_steps=n_steps,
                             past_len=n_pages * PAGE_SIZE, t_new=t_new)
    grid_spec = pltpu.PrefetchScalarGridSpec(
        num_scalar_prefetch=1,
        grid=(n_batch, n_steps),
        in_specs=[tok_spec,
                  pl.BlockSpec((D_MODEL, n_tok), lambda b, p, pt: (0, 0)),
                  pl.BlockSpec((n_tok, D_MODEL), lambda b, p, pt: (0, 0)),
                  tok_spec,
                  pl.BlockSpec((1, LANE), lambda b, p, pt: (0, 0)), vec, vec, vec, vec]
        + [page_spec(r) for r in range(pages)] + [page_spec(r) for r in range(pages)],
        out_specs=tok_spec,
        scratch_shapes=[
            pltpu.VMEM((ATT_HEADS, n_rows, LANE), BF16),
            pltpu.VMEM((ATT_HEADS, pages * PAGE_SIZE, LANE), BF16),
            pltpu.VMEM((ATT_HEADS, n_rows, LANE), F32),
            pltpu.VMEM((ATT_HEADS, n_rows, LANE), F32),
            pltpu.VMEM((ATT_HEADS, n_rows, LANE), F32),
            pltpu.VMEM((ATT_HEADS, n_rows, LANE), F32),
        ],
    )
    return pl.pallas_call(
        kern,
        grid_spec=grid_spec,
        out_shape=jax.ShapeDtypeStruct((n_batch, t_new, D_MODEL), F32),
        compiler_params=_params(("arbitrary", "arbitrary")),
        name="attn_sample",
    )(page_table, q, knt, v_new, gate, subln_g.reshape(1, LANE),
      *[t.reshape(1, ATT_HEAD_DIM) for t in lams], *([cache_kt] * pages), *([cache_v2] * pages))


def _conv_in_kernel(*refs, nb, tb, tiles_per_batch, has_state):
    if has_state:
        (x_ref, ng_ref, wa_ref, wl_ref, wg_ref, cw_ref, cb_ref, st_in_ref,
         y_ref, gate_ref, st_ref, h_s, ext_s, wb_s) = refs
        carry_s = None
    else:
        (x_ref, ng_ref, wa_ref, wl_ref, wg_ref, cw_ref, cb_ref,
         y_ref, gate_ref, st_ref, h_s, ext_s, wb_s, carry_s) = refs
    i = pl.program_id(0)
    cb = pl.program_id(1)

    @pl.when(cb == 0)
    def _():
        _rms_rows(x_ref, ng_ref, h_s)

    a = jnp.dot(h_s[...], wa_ref[...], preferred_element_type=F32)
    glu = jnp.dot(h_s[...], wl_ref[...], preferred_element_type=F32)
    c = a * jax.nn.sigmoid(glu)

    head0 = CONV_HEAD - CONV_STATE
    if has_state:
        ext_s[:, 0:head0, :] = jnp.zeros((nb, head0, COL_BLOCK), F32)
        ext_s[:, head0:CONV_HEAD, :] = st_in_ref[...]
    else:
        first = (i % tiles_per_batch) == 0

        @pl.when(first)
        def _():
            ext_s[0, 0:CONV_HEAD, :] = jnp.zeros((CONV_HEAD, COL_BLOCK), F32)

        @pl.when(jnp.logical_not(first))
        def _():
            ext_s[0, 0:CONV_HEAD, :] = carry_s[cb]

    for b in range(nb):
        ext_s[b, CONV_HEAD:CONV_HEAD + tb, :] = c[b * tb:(b + 1) * tb, :]

    rc = min(tb, ROW_CHUNK)
    n_lane = COL_BLOCK // LANE_CHUNK
    n_row = tb // rc

    for j in range(CONV_WIDTH):
        wb_s[j] = jnp.broadcast_to(cw_ref[j:j + 1, :], (SUBLANES, COL_BLOCK))

    def chunk(it, carry):
        b = it // (n_row * n_lane)
        r0 = pl.multiple_of(((it // n_lane) % n_row) * rc, rc)
        l0 = pl.multiple_of((it % n_lane) * LANE_CHUNK, LANE_CHUNK)
        blk = ext_s[b, pl.ds(r0, rc + CONV_HEAD), pl.ds(l0, LANE_CHUNK)]
        acc = jnp.broadcast_to(cb_ref[:, pl.ds(l0, LANE_CHUNK)], (rc, LANE_CHUNK))
        for r in range(SUBLANES):
            shifted = blk if r == 0 else blk[r:r + rc + CONV_HEAD - SUBLANES, :]
            for j in range(CONV_WIDTH):
                if (head0 + j) % SUBLANES == r:
                    a = head0 + j - r
                    w_j = pltpu.repeat(wb_s[j, :, pl.ds(l0, LANE_CHUNK)], rc // SUBLANES, axis=0)
                    acc = acc + w_j * shifted[a:a + rc, :]
        y_ref[pl.ds(b * tb + r0, rc), pl.ds(l0, LANE_CHUNK)] = acc
        return carry

    lax.fori_loop(0, nb * n_row * n_lane, chunk, 0, unroll=2)
    col0 = pl.multiple_of(cb * COL_BLOCK, COL_BLOCK)
    if has_state:
        for b in range(nb):
            st_ref[b, :, pl.ds(col0, COL_BLOCK)] = ext_s[b, tb + head0:tb + CONV_HEAD, :]
    else:
        @pl.when((i % tiles_per_batch) == tiles_per_batch - 1)
        def _():
            st_ref[i // tiles_per_batch, :, pl.ds(col0, COL_BLOCK)] = ext_s[0, tb + head0:tb + CONV_HEAD, :]

        carry_s[cb] = ext_s[0, tb:tb + CONV_HEAD, :]

    gate_ref[...] = jnp.dot(h_s[...], wg_ref[...], preferred_element_type=F32).astype(BF16)


def _conv_in(x2d, norm_g, w_in, conv_w, conv_b, state, *, n_batch, seq, tm):
    m = x2d.shape[0]
    has_state = state is not None
    if has_state:
        assert tm == m
        nb, tb, tiles_per_batch = n_batch, seq, 1
    else:
        assert seq % tm == 0 and tm >= CONV_HEAD
        nb, tb, tiles_per_batch = 1, tm, seq // tm
    n_cb = D_INNER // COL_BLOCK
    in_specs = [
        pl.BlockSpec((tm, D_MODEL), lambda i, c: (i, 0)),
        pl.BlockSpec((1, D_MODEL), lambda i, c: (0, 0)),
        pl.BlockSpec((D_MODEL, COL_BLOCK), lambda i, c: (0, c)),
        pl.BlockSpec((D_MODEL, COL_BLOCK), lambda i, c: (0, n_cb + c)),
        pl.BlockSpec((D_MODEL, COL_BLOCK), lambda i, c: (0, 2 * n_cb + c)),
        pl.BlockSpec((CONV_WIDTH, COL_BLOCK), lambda i, c: (0, c)),
        pl.BlockSpec((1, COL_BLOCK), lambda i, c: (0, c)),
    ]
    args = [x2d, norm_g.reshape(1, D_MODEL), w_in, w_in, w_in, conv_w, conv_b.reshape(1, D_INNER)]
    scratch = [
        pltpu.VMEM((tm, D_MODEL), BF16),
        pltpu.VMEM((nb, CONV_HEAD + tb, COL_BLOCK), F32),
        pltpu.VMEM((CONV_WIDTH, SUBLANES, COL_BLOCK), F32),
    ]
    if has_state:
        in_specs.append(pl.BlockSpec((nb, CONV_STATE, COL_BLOCK), lambda i, c: (0, 0, c)))
        args.append(state)
    else:
        scratch.append(pltpu.VMEM((n_cb, CONV_HEAD, COL_BLOCK), F32))
    kern = functools.partial(_conv_in_kernel, nb=nb, tb=tb, tiles_per_batch=tiles_per_batch,
                             has_state=has_state)
    return pl.pallas_call(
        kern,
        grid=(m // tm, n_cb),
        in_specs=in_specs,
        out_specs=[
            pl.BlockSpec((tm, COL_BLOCK), lambda i, c: (i, c)),
            pl.BlockSpec((tm, COL_BLOCK), lambda i, c: (i, c)),
            pl.BlockSpec((n_batch, CONV_STATE, D_INNER), lambda i, c: (0, 0, 0)),
        ],
        out_shape=[
            jax.ShapeDtypeStruct((m, D_INNER), F32),
            jax.ShapeDtypeStruct((m, D_INNER), BF16),
            jax.ShapeDtypeStruct((n_batch, CONV_STATE, D_INNER), F32),
        ],
        scratch_shapes=scratch,
        compiler_params=_params(("arbitrary", "arbitrary")),
        name="conv_in",
    )(*args)


PROMPT_TM = 512
OUT_TM = 1024
LN_TM = 512
LN_TN = 512
ATTN_TQ = 1024
ATTN_TK = 1024
SAMPLE_PAGES_PER_STEP = 8


def kernel(x_prompt, x_sample, state_pool_l0, cache_k_l1, cache_v_l1, state_conv_l2, state_pool_l3, page_table, norm_g_l0, w_in_l0, w_grp_l0, pool_scale_l0, w_out_l0, norm_g_l1, w_in_l1, q_norm_g_l1, k_norm_g_l1, lambda_q1_l1, lambda_k1_l1, lambda_q2_l1, lambda_k2_l1, subln_g_l1, w_out_l1, norm_g_l2, w_in_l2, conv_w_l2, conv_b_l2, ln_g_l2, ln_b_l2, w_out_l2, norm_g_l3, w_in_l3, w_grp_l3, pool_scale_l3, w_out_l3):
    nbp, seq, _ = x_prompt.shape
    nbs, t_new, _ = x_sample.shape
    mp, ms = nbp * seq, nbs * t_new
    xp = x_prompt.reshape(mp, D_MODEL)
    xs = x_sample.reshape(ms, D_MODEL)
    lams = (lambda_q1_l1, lambda_k1_l1, lambda_q2_l1, lambda_k2_l1)

    def bf(w):
        return w.astype(BF16)

    def pool_layer(xp, xs, state, norm_g, w_in, w_grp, scale, w_out):
        w_in, w_grp, w_out = bf(w_in), bf(w_grp), bf(w_out)
        zp, stp = _pool_in(xp, norm_g, w_in, w_grp, scale, None, n_batch=nbp, seq=seq, tm=PROMPT_TM)
        zs, sts = _pool_in(xs, norm_g, w_in, w_grp, scale, state, n_batch=nbs, seq=t_new, tm=ms)
        xp = _out_proj(zp, w_out, xp, tm=OUT_TM, tn=COL_BLOCK)
        xs = _out_proj(zs, w_out, xs, tm=ms, tn=COL_BLOCK)
        return xp, xs, stp, sts

    xp, xs, pool0_p, pool0_s = pool_layer(xp, xs, state_pool_l0, norm_g_l0, w_in_l0, w_grp_l0,
                                          pool_scale_l0, w_out_l0)

    w_in, w_out = bf(w_in_l1), bf(w_out_l1)
    qp, ktp, vp, vnp, gp = _attn_in(xp, norm_g_l1, w_in, q_norm_g_l1, k_norm_g_l1, n_batch=nbp, seq=seq,
                                    tm=PROMPT_TM, q_dtype=BF16, v_dtype=BF16, native_v=True)
    qs, kts, vs, gs = _attn_in(xs, norm_g_l1, w_in, q_norm_g_l1, k_norm_g_l1, n_batch=1, seq=ms,
                               tm=ms, q_dtype=F32, v_dtype=F32, native_v=False)
    shp = (nbp, seq, D_MODEL)
    zp = _attn_prompt(qp.reshape(shp), ktp, vp.reshape(shp), gp.reshape(shp), subln_g_l1, lams,
                      n_batch=nbp, seq=seq, tq=ATTN_TQ, tk=ATTN_TK)
    n_phys = cache_k_l1.shape[0]
    cache_kt = jnp.transpose(cache_k_l1, (0, 2, 3, 4, 1)).reshape(n_phys, D_MODEL, PAGE_SIZE)
    cache_v2 = cache_v_l1.reshape(n_phys, PAGE_SIZE * ATT_HEADS, ATT_VDIM)
    shs = (nbs, t_new, D_MODEL)
    zs = _attn_sample(qs.reshape(shs), kts[0], vs, gs.reshape(shs), cache_kt, cache_v2, page_table,
                      subln_g_l1, lams, pages=SAMPLE_PAGES_PER_STEP)
    xp = _out_proj(zp.reshape(mp, D_MODEL), w_out, xp, tm=OUT_TM, tn=COL_BLOCK)
    xs = _out_proj(zs.reshape(ms, D_MODEL).astype(BF16), w_out, xs, tm=ms, tn=COL_BLOCK)
    new_k_p = jnp.transpose(ktp.reshape(nbp, ATT_HEADS, 2, ATT_HEAD_DIM, seq), (0, 4, 1, 2, 3))
    new_v_p = vnp.reshape(nbp, seq, ATT_HEADS, ATT_VDIM)
    new_k_s = kts[0].T.reshape(nbs, t_new, ATT_HEADS, 2, ATT_HEAD_DIM)
    new_v_s = vs.reshape(nbs, t_new, ATT_HEADS, ATT_VDIM)

    w_in, w_out = bf(w_in_l2), bf(w_out_l2)
    yp, gp, conv_p = _conv_in(xp, norm_g_l2, w_in, conv_w_l2, conv_b_l2, None, n_batch=nbp, seq=seq,
                              tm=PROMPT_TM)
    ys, gs, conv_s = _conv_in(xs, norm_g_l2, w_in, conv_w_l2, conv_b_l2, state_conv_l2, n_batch=nbs,
                              seq=t_new, tm=ms)
    xp = _ln_out_proj(yp, gp, ln_g_l2, ln_b_l2, w_out, xp, tm=LN_TM, tn=LN_TN)
    xs = _ln_out_proj(ys, gs, ln_g_l2, ln_b_l2, w_out, xs, tm=ms, tn=LN_TN)

    xp, xs, pool3_p, pool3_s = pool_layer(xp, xs, state_pool_l3, norm_g_l3, w_in_l3, w_grp_l3,
                                          pool_scale_l3, w_out_l3)

    return (xp.reshape(nbp, seq, D_MODEL), xs.reshape(nbs, t_new, D_MODEL), pool0_p, pool0_s,
            new_k_p, new_v_p, new_k_s, new_v_s, conv_p, conv_s, pool3_p, pool3_s)
```

```python
import functools
import math

import jax
import jax.numpy as jnp
from jax import lax
from jax.experimental import pallas as pl
from jax.experimental.pallas import tpu as pltpu

F32 = jnp.float32
BF16 = jnp.bfloat16

D_MODEL = 2048
D_INNER = 4096
POOL_WINDOWS = (2, 4, 8, 16)
POOL_GROUP = 1024
POOL_STATE = 15
ATT_HEADS = 16
ATT_HEAD_DIM = 64
ATT_VDIM = 128
LAMBDA_INIT = 0.8 - 0.6 * math.exp(-0.3 * 1)
LOG2E = math.log2(math.e)
QK_SCALE = ATT_HEAD_DIM ** -0.5 * LOG2E
CONV_WIDTH = 31
CONV_STATE = 30
NORM_EPS = 1e-6
NEG_INF = -1e30
PAGE_SIZE = 128

VMEM_LIMIT_BYTES = 58 * 1024 * 1024
LANE = 128
SUBLANES = 8
BF16_ROWS = 16
COL_BLOCK = 1024
POOL_HEAD = 16
CONV_HEAD = 32
LANE_CHUNK = 256
CONV_ROWS = 64
CONV_LANES = 128


def _params(sem):
    return pltpu.CompilerParams(dimension_semantics=sem, vmem_limit_bytes=VMEM_LIMIT_BYTES)


def _rms_rows(x_ref, g_ref, h_ref):
    xf = x_ref[...]
    ms = jnp.mean(xf * xf, axis=-1, keepdims=True)
    h_ref[...] = (xf * lax.rsqrt(ms + NORM_EPS) * g_ref[...]).astype(BF16)


def _silu(x):
    return x * jax.nn.sigmoid(x)


def _pool_rows(blk, w, n_seen):
    rows = blk.shape[0] - POOL_HEAD
    cur = blk[POOL_HEAD:, :]
    acc = cur
    for j in range(1, w):
        acc = acc + blk[POOL_HEAD - j:POOL_HEAD - j + rows, :]
    cnt = jnp.minimum(n_seen, w).astype(F32)
    return jnp.where(n_seen >= w, acc * (1.0 / w), acc / cnt) - cur


def _pool_finish(h_s, pooled, wg_ref, wgrp_ref, sc_ref, z_ref):
    mixed = jnp.dot(pooled, wgrp_ref[0], preferred_element_type=F32) * sc_ref[...]
    gate = jnp.dot(h_s[...], wg_ref[...], preferred_element_type=F32)
    z_ref[...] = (mixed * _silu(gate)).astype(BF16)


def _pool_in_prompt_kernel(x_ref, ng_ref, wu_ref, wg_ref, wgrp_ref, sc_ref, pmat_ref,
                           z_ref, st_ref, h_s, edge_s, pooled_s, carry_s, *, tm, tiles_per_batch):
    i = pl.program_id(0)
    g = pl.program_id(1)

    @pl.when(g == 0)
    def _():
        _rms_rows(x_ref, ng_ref, h_s)

    u = jnp.dot(h_s[...], wu_ref[...], preferred_element_type=F32)
    first = (i % tiles_per_batch) == 0

    pooled_s[...] = jnp.dot(pmat_ref[0], u.astype(BF16), preferred_element_type=F32).astype(BF16)

    @pl.when(first)
    def _():
        edge_s[0:POOL_HEAD, :] = jnp.zeros((POOL_HEAD, COL_BLOCK), F32)

    @pl.when(jnp.logical_not(first))
    def _():
        edge_s[0:POOL_HEAD, :] = carry_s[g]

    edge_s[POOL_HEAD:, :] = u[0:POOL_HEAD, :]
    n_seen = ((i % tiles_per_batch) * tm + 1
              + lax.broadcasted_iota(jnp.int32, (POOL_HEAD, LANE_CHUNK), 0))
    for k, w in enumerate(POOL_WINDOWS):
        @pl.when(g == k)
        def _(w=w):
            for l0 in range(0, COL_BLOCK, LANE_CHUNK):
                pooled_s[0:POOL_HEAD, l0:l0 + LANE_CHUNK] = _pool_rows(
                    edge_s[:, l0:l0 + LANE_CHUNK], w, n_seen).astype(BF16)

    @pl.when((i % tiles_per_batch) == tiles_per_batch - 1)
    def _():
        col0 = pl.multiple_of(g * COL_BLOCK, COL_BLOCK)
        st_ref[i // tiles_per_batch, :, pl.ds(col0, COL_BLOCK)] = u[tm - POOL_STATE:tm, :]

    carry_s[g] = u[tm - POOL_HEAD:tm, :]
    _pool_finish(h_s, pooled_s[...], wg_ref, wgrp_ref, sc_ref, z_ref)


def _pool_in_sample_kernel(x_ref, ng_ref, wu_ref, wg_ref, wgrp_ref, sc_ref, st_in_ref,
                           z_ref, st_ref, h_s, ext_s, pooled_s, *, nb, tb):
    g = pl.program_id(1)

    @pl.when(g == 0)
    def _():
        _rms_rows(x_ref, ng_ref, h_s)

    u = jnp.dot(h_s[...], wu_ref[...], preferred_element_type=F32)

    ext_s[:, 0:1, :] = jnp.zeros((nb, 1, COL_BLOCK), F32)
    ext_s[:, 1:POOL_HEAD, :] = st_in_ref[...]
    for b in range(nb):
        ext_s[b, POOL_HEAD:, :] = u[b * tb:(b + 1) * tb, :]

    n_seen = POOL_STATE + 1 + lax.broadcasted_iota(jnp.int32, (tb, LANE_CHUNK), 0)
    for k, w in enumerate(POOL_WINDOWS):
        @pl.when(g == k)
        def _(w=w):
            for b in range(nb):
                for l0 in range(0, COL_BLOCK, LANE_CHUNK):
                    pooled_s[b * tb:(b + 1) * tb, l0:l0 + LANE_CHUNK] = _pool_rows(
                        ext_s[b, :, l0:l0 + LANE_CHUNK], w, n_seen)

    col0 = pl.multiple_of(g * COL_BLOCK, COL_BLOCK)
    for b in range(nb):
        st_ref[b, :, pl.ds(col0, COL_BLOCK)] = ext_s[b, tb + 1:tb + POOL_HEAD, :]
    _pool_finish(h_s, pooled_s[...].astype(BF16), wg_ref, wgrp_ref, sc_ref, z_ref)


def _pool_matrices(tm):
    t = jnp.arange(tm)[:, None]
    j = jnp.arange(tm)[None, :]
    mats = [jnp.where((t - j >= 0) & (t - j < w), 1.0 / w, 0.0) - (t == j) for w in POOL_WINDOWS]
    return jnp.stack(mats).astype(BF16)


def _pool_in(x2d, norm_g, w_in, w_grp, scale, state, *, n_batch, seq, tm):
    m = x2d.shape[0]
    n_groups = len(POOL_WINDOWS)
    in_specs = [
        pl.BlockSpec((tm, D_MODEL), lambda i, g: (i, 0)),
        pl.BlockSpec((1, D_MODEL), lambda i, g: (0, 0)),
        pl.BlockSpec((D_MODEL, COL_BLOCK), lambda i, g: (0, g)),
        pl.BlockSpec((D_MODEL, COL_BLOCK), lambda i, g: (0, n_groups + g)),
        pl.BlockSpec((1, POOL_GROUP, POOL_GROUP), lambda i, g: (g, 0, 0)),
        pl.BlockSpec((1, COL_BLOCK), lambda i, g: (0, g)),
    ]
    args = [x2d, norm_g.reshape(1, D_MODEL), w_in, w_in, w_grp, scale.reshape(1, D_INNER)]
    if state is None:
        assert seq % tm == 0 and tm >= POOL_HEAD
        in_specs.append(pl.BlockSpec((1, tm, tm), lambda i, g: (g, 0, 0)))
        args.append(_pool_matrices(tm))
        scratch = [
            pltpu.VMEM((tm, D_MODEL), BF16),
            pltpu.VMEM((2 * POOL_HEAD, COL_BLOCK), F32),
            pltpu.VMEM((tm, COL_BLOCK), BF16),
            pltpu.VMEM((n_groups, POOL_HEAD, COL_BLOCK), F32),
        ]
        kern = functools.partial(_pool_in_prompt_kernel, tm=tm, tiles_per_batch=seq // tm)
    else:
        assert tm == m
        in_specs.append(pl.BlockSpec((n_batch, POOL_STATE, COL_BLOCK), lambda i, g: (0, 0, g)))
        args.append(state)
        scratch = [
            pltpu.VMEM((tm, D_MODEL), BF16),
            pltpu.VMEM((n_batch, POOL_HEAD + seq, COL_BLOCK), F32),
            pltpu.VMEM((tm, COL_BLOCK), F32),
        ]
        kern = functools.partial(_pool_in_sample_kernel, nb=n_batch, tb=seq)
    return pl.pallas_call(
        kern,
        grid=(m // tm, n_groups),
        in_specs=in_specs,
        out_specs=[
            pl.BlockSpec((tm, COL_BLOCK), lambda i, g: (i, g)),
            pl.BlockSpec((n_batch, POOL_STATE, D_INNER), lambda i, g: (0, 0, 0)),
        ],
        out_shape=[
            jax.ShapeDtypeStruct((m, D_INNER), BF16),
            jax.ShapeDtypeStruct((n_batch, POOL_STATE, D_INNER), F32),
        ],
        scratch_shapes=scratch,
        compiler_params=_params(("arbitrary", "arbitrary")),
        name="pool_in",
    )(*args)


def _out_proj_kernel(z_ref, w_ref, x_ref, o_ref):
    o_ref[...] = x_ref[...] + jnp.dot(z_ref[...], w_ref[...], preferred_element_type=F32)


def _out_proj(z, w, x2d, *, tm, tn):
    m, k = z.shape
    n = w.shape[1]
    return pl.pallas_call(
        _out_proj_kernel,
        grid=(m // tm, n // tn),
        in_specs=[
            pl.BlockSpec((tm, k), lambda i, j: (i, 0)),
            pl.BlockSpec((k, tn), lambda i, j: (0, j)),
            pl.BlockSpec((tm, tn), lambda i, j: (i, j)),
        ],
        out_specs=pl.BlockSpec((tm, tn), lambda i, j: (i, j)),
        out_shape=jax.ShapeDtypeStruct((m, n), F32),
        compiler_params=_params(("arbitrary", "arbitrary")),
        name="out_proj",
    )(z, w, x2d)


def _ln_out_proj_kernel(y_ref, gate_ref, lg_ref, lb_ref, w_ref, x_ref, o_ref, z_s):
    @pl.when(pl.program_id(1) == 0)
    def _():
        def rows(it, carry):
            r0 = pl.multiple_of(it * BF16_ROWS, BF16_ROWS)
            y = y_ref[pl.ds(r0, BF16_ROWS), :]
            yc = y - jnp.mean(y, axis=-1, keepdims=True)
            var = jnp.mean(yc * yc, axis=-1, keepdims=True)
            ln = yc * lax.rsqrt(var + NORM_EPS) * lg_ref[...] + lb_ref[...]
            gate = gate_ref[pl.ds(r0, BF16_ROWS), :].astype(F32)
            z_s[pl.ds(r0, BF16_ROWS), :] = (_silu(ln) * _silu(gate)).astype(BF16)
            return carry

        lax.fori_loop(0, z_s.shape[0] // BF16_ROWS, rows, 0, unroll=4)

    o_ref[...] = x_ref[...] + jnp.dot(z_s[...], w_ref[...], preferred_element_type=F32)


def _ln_out_proj(y, gate, ln_g, ln_b, w, x2d, *, tm, tn):
    m, k = y.shape
    n = w.shape[1]
    return pl.pallas_call(
        _ln_out_proj_kernel,
        grid=(m // tm, n // tn),
        in_specs=[
            pl.BlockSpec((tm, k), lambda i, j: (i, 0)),
            pl.BlockSpec((tm, k), lambda i, j: (i, 0)),
            pl.BlockSpec((1, k), lambda i, j: (0, 0)),
            pl.BlockSpec((1, k), lambda i, j: (0, 0)),
            pl.BlockSpec((k, tn), lambda i, j: (0, j)),
            pl.BlockSpec((tm, tn), lambda i, j: (i, j)),
        ],
        out_specs=pl.BlockSpec((tm, tn), lambda i, j: (i, j)),
        out_shape=jax.ShapeDtypeStruct((m, n), F32),
        scratch_shapes=[pltpu.VMEM((tm, k), BF16)],
        compiler_params=_params(("arbitrary", "arbitrary")),
        name="ln_out_proj",
    )(y, gate, ln_g.reshape(1, k), ln_b.reshape(1, k), w, x2d)


def _head_norm(y, g128):
    lane = lax.broadcasted_iota(jnp.int32, (1, LANE), 1)
    low = lane < ATT_HEAD_DIM
    outs = []
    for c in range(y.shape[1] // LANE):
        blk = y[:, c * LANE:(c + 1) * LANE]
        sq = blk * blk
        s_lo = jnp.sum(jnp.where(low, sq, 0.0), axis=-1, keepdims=True)
        s_hi = jnp.sum(jnp.where(low, 0.0, sq), axis=-1, keepdims=True)
        ms = jnp.where(low, s_lo, s_hi) * (1.0 / ATT_HEAD_DIM)
        outs.append(blk * lax.rsqrt(ms + NORM_EPS) * g128)
    return jnp.concatenate(outs, axis=1)


def _attn_in_kernel(x_ref, ng_ref, w_ref, qg_ref, kg_ref, q_ref, kt_ref, v_ref, vn_ref, gate_ref, h_s,
                    *, q_dtype, v_dtype, tm):
    j = pl.program_id(1)

    @pl.when(j == 0)
    def _():
        _rms_rows(x_ref, ng_ref, h_s)

    @pl.when(j < 2)
    def _():
        y = jnp.dot(h_s[...], w_ref[...], preferred_element_type=F32)
        q_ref[...] = (_head_norm(y, qg_ref[...]) * QK_SCALE).astype(q_dtype)

    @pl.when(jnp.logical_and(j >= 2, j < 4))
    def _():
        yt = lax.dot_general(w_ref[...], h_s[...], (((0,), (1,)), ((), ())), preferred_element_type=F32)
        y3 = yt.reshape(COL_BLOCK // ATT_HEAD_DIM, ATT_HEAD_DIM, tm)
        ms = jnp.mean(y3 * y3, axis=1, keepdims=True)
        kt_ref[...] = (y3 * lax.rsqrt(ms + NORM_EPS) * kg_ref[...][None]).reshape(COL_BLOCK, tm)

    for jj in (4, 5):
        @pl.when(j == jj)
        def _(jj=jj):
            y = jnp.dot(h_s[...], w_ref[...], preferred_element_type=F32)
            v_ref[...] = y.astype(v_dtype)
            if vn_ref is not None:
                for hl in range(COL_BLOCK // LANE):
                    hh = (jj - 4) * (COL_BLOCK // LANE) + hl
                    vn_ref[pl.ds(hh, tm, stride=ATT_HEADS), :] = y[:, hl * LANE:(hl + 1) * LANE]

    @pl.when(j >= 6)
    def _():
        gate_ref[...] = jnp.dot(h_s[...], w_ref[...], preferred_element_type=F32).astype(gate_ref.dtype)


def _attn_in(x2d, norm_g, w_in, q_norm_g, k_norm_g, *, n_batch, seq, tm, q_dtype, v_dtype, native_v):
    m = x2d.shape[0]
    tiles_per_batch = seq // tm
    n_col = w_in.shape[1] // COL_BLOCK

    def out_spec(first):
        return pl.BlockSpec((tm, COL_BLOCK), lambda i, j: (i, jnp.clip(j - first, 0, 1)))

    out_specs = [
        out_spec(0),
        pl.BlockSpec((None, COL_BLOCK, tm),
                     lambda i, j: (i // tiles_per_batch, jnp.clip(j - 2, 0, 1), i % tiles_per_batch)),
        out_spec(4),
    ]
    out_shape = [
        jax.ShapeDtypeStruct((m, D_MODEL), q_dtype),
        jax.ShapeDtypeStruct((n_batch, D_MODEL, seq), F32),
        jax.ShapeDtypeStruct((m, D_MODEL), v_dtype),
    ]
    if native_v:
        out_specs.append(pl.BlockSpec((tm * ATT_HEADS, LANE), lambda i, j: (i, 0)))
        out_shape.append(jax.ShapeDtypeStruct((m * ATT_HEADS, LANE), F32))
    out_specs.append(out_spec(6))
    out_shape.append(jax.ShapeDtypeStruct((m, D_MODEL), v_dtype))

    def kern(*refs):
        ins, outs, scratch = refs[:5], refs[5:-1], refs[-1]
        if native_v:
            q_ref, kt_ref, v_ref, vn_ref, gate_ref = outs
        else:
            (q_ref, kt_ref, v_ref, gate_ref), vn_ref = outs, None
        _attn_in_kernel(*ins, q_ref, kt_ref, v_ref, vn_ref, gate_ref, scratch,
                        q_dtype=q_dtype, v_dtype=v_dtype, tm=tm)

    return pl.pallas_call(
        kern,
        grid=(m // tm, n_col),
        in_specs=[
            pl.BlockSpec((tm, D_MODEL), lambda i, j: (i, 0)),
            pl.BlockSpec((1, D_MODEL), lambda i, j: (0, 0)),
            pl.BlockSpec((D_MODEL, COL_BLOCK), lambda i, j: (0, j)),
            pl.BlockSpec((1, LANE), lambda i, j: (0, 0)),
            pl.BlockSpec((ATT_HEAD_DIM, 1), lambda i, j: (0, 0)),
        ],
        out_specs=out_specs,
        out_shape=out_shape,
        scratch_shapes=[pltpu.VMEM((tm, D_MODEL), BF16)],
        compiler_params=_params(("arbitrary", "arbitrary")),
        name="attn_in",
    )(x2d, norm_g.reshape(1, D_MODEL), w_in, jnp.tile(q_norm_g, 2).reshape(1, LANE),
      k_norm_g.reshape(ATT_HEAD_DIM, 1))


def _lambda(lq1_ref, lk1_ref, lq2_ref, lk2_ref):
    a = jnp.sum(lq1_ref[...] * lk1_ref[...], axis=-1, keepdims=True)
    b = jnp.sum(lq2_ref[...] * lk2_ref[...], axis=-1, keepdims=True)
    return jnp.exp(a) - jnp.exp(b) + LAMBDA_INIT


def _sub_ln_gate(o, subln_g, gate):
    ms = jnp.mean(o * o, axis=-1, keepdims=True)
    o = o * lax.rsqrt(ms + NORM_EPS) * subln_g * (1.0 - LAMBDA_INIT)
    return o * _silu(gate)


def _attn_prompt_kernel(q_ref, kt_ref, v_ref, gate_ref, sg_ref, lq1_ref, lk1_ref, lq2_ref, lk2_ref,
                        z_ref, ka_s, m_s, l_s, acc_s, *, tq, tk):
    h = pl.program_id(1)
    qi = pl.program_id(2)
    seq = kt_ref.shape[1]
    slope2 = jnp.exp2(-0.5 * jnp.full((1, LANE), h + 1, jnp.int32).astype(F32)) * LOG2E

    @pl.when(qi == 0)
    def _():
        col = lax.broadcasted_iota(jnp.int32, (BF16_ROWS, seq), 1) % tk
        row = lax.broadcasted_iota(jnp.int32, (BF16_ROWS, seq), 0)
        b = pltpu.repeat(slope2, seq // LANE, axis=1) * col.astype(F32)
        hi = b.astype(BF16).astype(F32)
        mid = (b - hi).astype(BF16).astype(F32)
        lo = b - hi - mid
        rows = jnp.where(row == 0, hi, jnp.where(row == 1, mid, jnp.where(row == 2, lo, 0.0))).astype(BF16)
        zeros = jnp.zeros((ATT_HEAD_DIM - BF16_ROWS, seq), BF16)
        ka_s[0, 0:ATT_HEAD_DIM, :] = kt_ref[0:ATT_HEAD_DIM, :].astype(BF16)
        ka_s[0, ATT_HEAD_DIM:ATT_HEAD_DIM + BF16_ROWS, :] = rows
        ka_s[0, ATT_HEAD_DIM + BF16_ROWS:, :] = zeros
        ka_s[1, 0:BF16_ROWS, :] = rows
        ka_s[1, BF16_ROWS:ATT_HEAD_DIM, :] = zeros
        ka_s[1, ATT_HEAD_DIM:, :] = kt_ref[ATT_HEAD_DIM:, :].astype(BF16)

    q = q_ref[...]
    lane = lax.broadcasted_iota(jnp.int32, (1, LANE), 1)
    low = lane < ATT_HEAD_DIM
    zero = jnp.zeros_like(q)
    ones1 = jnp.where(jnp.logical_and(lane >= ATT_HEAD_DIM, lane < ATT_HEAD_DIM + 3), 1.0, 0.0).astype(BF16)
    ones2 = jnp.where(lane < 3, 1.0, 0.0).astype(BF16)
    qa = (jnp.where(low, q, zero) + ones1, jnp.where(low, zero, q) + ones2)

    m_s[...] = jnp.full(m_s.shape, NEG_INF, F32)
    l_s[...] = jnp.zeros(l_s.shape, F32)
    acc_s[...] = jnp.zeros(acc_s.shape, F32)

    def tile(kj, masked, r0=0, r1=tq, n_col=tk):
        start = pl.multiple_of(kj * tk, tk)
        vt = v_ref[pl.ds(start, n_col), :]
        shift = slope2 * jnp.full((1, LANE), kj * tk - qi * tq, jnp.int32).astype(F32)
        for c in range(2):
            s = jnp.dot(qa[c][r0:r1], ka_s[c, :, pl.ds(start, n_col)], preferred_element_type=F32)
            if masked:
                row_i = lax.broadcasted_iota(jnp.int32, (r1 - r0, n_col), 0) + r0
                col_j = lax.broadcasted_iota(jnp.int32, (r1 - r0, n_col), 1) + (kj * tk - qi * tq)
                s = jnp.where(col_j <= row_i, s, NEG_INF)
            m_old = m_s[c, r0:r1]
            m_new = jnp.maximum(m_old, jnp.max(s, axis=-1, keepdims=True) + shift)
            alpha = jnp.exp2(m_old - m_new)
            p = jnp.exp2(s - pltpu.repeat(m_new - shift, n_col // LANE, axis=1))
            l_s[c, r0:r1] = alpha * l_s[c, r0:r1] + jnp.sum(p, axis=-1, keepdims=True)
            acc_s[c, r0:r1] = alpha * acc_s[c, r0:r1] + jnp.dot(p.astype(BF16), vt, preferred_element_type=F32)
            m_s[c, r0:r1] = m_new

    n_full = (qi * tq) // tk

    def body(kj, carry):
        tile(kj, False)
        return carry

    lax.fori_loop(0, n_full, body, 0)
    if tq == tk:
        tile(n_full, True, 0, tq // 2, tk // 2)
        tile(n_full, True, tq // 2, tq, tk)
    else:
        for d in range(tq // tk):
            tile(n_full + d, True)

    lam = _lambda(lq1_ref, lk1_ref, lq2_ref, lk2_ref)
    o = acc_s[0] / l_s[0] - lam * (acc_s[1] / l_s[1])
    z_ref[...] = _sub_ln_gate(o, sg_ref[...], gate_ref[...].astype(F32)).astype(BF16)


def _attn_prompt(q, kt, v, gate, subln_g, lams, *, n_batch, seq, tq, tk):
    assert tq % tk == 0 and seq % tq == 0
    vec = pl.BlockSpec((1, ATT_HEAD_DIM), lambda b, h, i: (0, 0))
    kern = functools.partial(_attn_prompt_kernel, tq=tq, tk=tk)
    return pl.pallas_call(
        kern,
        grid=(n_batch, ATT_HEADS, seq // tq),
        in_specs=[
            pl.BlockSpec((None, tq, LANE), lambda b, h, i: (b, i, h)),
            pl.BlockSpec((None, LANE, seq), lambda b, h, i: (b, h, 0)),
            pl.BlockSpec((None, seq, LANE), lambda b, h, i: (b, 0, h)),
            pl.BlockSpec((None, tq, LANE), lambda b, h, i: (b, i, h)),
            pl.BlockSpec((1, LANE), lambda b, h, i: (0, 0)),
            vec, vec, vec, vec,
        ],
        out_specs=pl.BlockSpec((None, tq, LANE), lambda b, h, i: (b, i, h)),
        out_shape=jax.ShapeDtypeStruct((n_batch, seq, D_MODEL), BF16),
        scratch_shapes=[
            pltpu.VMEM((2, LANE, seq), BF16),
            pltpu.VMEM((2, tq, LANE), F32),
            pltpu.VMEM((2, tq, LANE), F32),
            pltpu.VMEM((2, tq, LANE), F32),
        ],
        compiler_params=_params(("arbitrary", "arbitrary", "arbitrary")),
        name="attn_prompt",
    )(q, kt, v, gate, subln_g.reshape(1, LANE), *[t.reshape(1, ATT_HEAD_DIM) for t in lams])


def _attn_sample_kernel(pt_ref, q_ref, knt_ref, vn_ref, gate_ref, sg_ref,
                        lq1_ref, lk1_ref, lq2_ref, lk2_ref, *rest, pages, n_steps, past_len, t_new):
    k_refs = rest[:pages]
    v_refs = rest[pages:2 * pages]
    z_ref, q3_s, vh_s, slope_s, m_s, l_s, acc_s = rest[2 * pages:]
    b = pl.program_id(0)
    p = pl.program_id(1)
    n_rows = 2 * t_new
    shape3 = (ATT_HEADS, n_rows, LANE)

    @pl.when(p == 0)
    def _():
        row_c = lax.broadcasted_iota(jnp.int32, (n_rows, LANE), 0) // t_new
        lane_c = lax.broadcasted_iota(jnp.int32, (n_rows, LANE), 1) // ATT_HEAD_DIM
        for hh in range(ATT_HEADS):
            blk = q_ref[:, hh * LANE:(hh + 1) * LANE]
            both = jnp.concatenate([blk, blk], axis=0)
            q3_s[hh] = jnp.where(row_c == lane_c, both, 0.0).astype(BF16)
        head = lax.broadcasted_iota(jnp.int32, shape3, 0)
        slope_s[...] = jnp.exp2(-0.5 * (head + 1).astype(F32)) * LOG2E
        m_s[...] = jnp.full(shape3, NEG_INF, F32)
        l_s[...] = jnp.zeros(shape3, F32)
        acc_s[...] = jnp.zeros(shape3, F32)

    q3 = q3_s[...]

    def update(s, vb):
        m_old = m_s[...]
        m_new = jnp.maximum(m_old, jnp.max(s, axis=-1, keepdims=True))
        alpha = jnp.exp2(m_old - m_new)
        t = s.shape[-1]
        m_all = pltpu.repeat(m_new, t // LANE, axis=2) if t % LANE == 0 else m_new[:, :, 0:t]
        pr = jnp.exp2(s - m_all)
        l_s[...] = alpha * l_s[...] + jnp.sum(pr, axis=-1, keepdims=True)
        pv = jnp.einsum("hrt,hte->hre", pr.astype(BF16), vb, preferred_element_type=F32)
        acc_s[...] = alpha * acc_s[...] + pv
        m_s[...] = m_new

    tok = lax.broadcasted_iota(jnp.int32, (1, 1, PAGE_SIZE), 2)
    slope = slope_s[...]
    scores = []
    for r in range(pages):
        k3 = k_refs[r][...].reshape(ATT_HEADS, LANE, PAGE_SIZE).astype(BF16)
        k_pos = ((p * pages + r) * PAGE_SIZE - past_len + tok).astype(F32)
        scores.append(jnp.einsum("hrk,hkt->hrt", q3, k3, preferred_element_type=F32) + slope * k_pos)
        v3 = v_refs[r][...].reshape(PAGE_SIZE, ATT_HEADS, LANE)
        vh_s[:, r * PAGE_SIZE:(r + 1) * PAGE_SIZE, :] = jnp.swapaxes(v3, 0, 1).astype(BF16)
    update(jnp.concatenate(scores, axis=-1), vh_s[...])

    @pl.when(p == n_steps - 1)
    def _():
        n_tok = knt_ref.shape[1]
        k3 = knt_ref[...].reshape(ATT_HEADS, LANE, n_tok).astype(BF16)
        tok_n = lax.broadcasted_iota(jnp.int32, (1, n_rows, n_tok), 2)
        row_t = lax.broadcasted_iota(jnp.int32, (1, n_rows, n_tok), 1) % t_new
        valid = jnp.logical_and(tok_n // t_new == b, tok_n % t_new <= row_t)
        s = jnp.einsum("hrk,hkt->hrt", q3, k3, preferred_element_type=F32)
        s = s + slope[:, :, 0:n_tok] * (tok_n % t_new).astype(F32)
        s = jnp.where(valid, s, NEG_INF)
        vb = jnp.stack([vn_ref[:, hh * LANE:(hh + 1) * LANE] for hh in range(ATT_HEADS)], axis=0).astype(BF16)
        update(s, vb)

        lam = _lambda(lq1_ref, lk1_ref, lq2_ref, lk2_ref)
        acc = acc_s[...]
        l = l_s[...]
        o = acc[:, 0:t_new, :] / l[:, 0:t_new, :] - lam * (acc[:, t_new:, :] / l[:, t_new:, :])
        for hh in range(ATT_HEADS):
            gate = gate_ref[:, hh * LANE:(hh + 1) * LANE]
            z_ref[:, hh * LANE:(hh + 1) * LANE] = _sub_ln_gate(o[hh], sg_ref[...], gate)


def _attn_sample(q, knt, v_new, gate, cache_kt, cache_v2, page_table, subln_g, lams, *, pages):
    n_batch, t_new, _ = q.shape
    n_pages = page_table.shape[1]
    n_steps = n_pages // pages
    n_rows = 2 * t_new
    n_tok = n_batch * t_new

    tok_spec = pl.BlockSpec((None, t_new, D_MODEL), lambda b, p, pt: (b, 0, 0))
    vec = pl.BlockSpec((1, ATT_HEAD_DIM), lambda b, p, pt: (0, 0))

    def page_spec(r):
        return pl.BlockSpec((None, D_MODEL, PAGE_SIZE), lambda b, p, pt: (pt[b, p * pages + r], 0, 0))

    kern = functools.partial(_attn_sample_kernel, pages=pages, n_steps=n_steps,
                             past_len=n_pages * PAGE_SIZE, t_new=t_new)
    grid_spec = pltpu.PrefetchScalarGridSpec(
        num_scalar_prefetch=1,
        grid=(n_batch, n_steps),
        in_specs=[tok_spec,
                  pl.BlockSpec((D_MODEL, n_tok), lambda b, p, pt: (0, 0)),
                  pl.BlockSpec((n_tok, D_MODEL), lambda b, p, pt: (0, 0)),
                  tok_spec,
                  pl.BlockSpec((1, LANE), lambda b, p, pt: (0, 0)), vec, vec, vec, vec]
        + [page_spec(r) for r in range(pages)] + [page_spec(r) for r in range(pages)],
        out_specs=tok_spec,
        scratch_shapes=[
            pltpu.VMEM((ATT_HEADS, n_rows, LANE), BF16),
            pltpu.VMEM((ATT_HEADS, pages * PAGE_SIZE, LANE), BF16),
            pltpu.VMEM((ATT_HEADS, n_rows, LANE), F32),
            pltpu.VMEM((ATT_HEADS, n_rows, LANE), F32),
            pltpu.VMEM((ATT_HEADS, n_rows, LANE), F32),
            pltpu.VMEM((ATT_HEADS, n_rows, LANE), F32),
        ],
    )
    return pl.pallas_call(
        kern,
        grid_spec=grid_spec,
        out_shape=jax.ShapeDtypeStruct((n_batch, t_new, D_MODEL), F32),
        compiler_params=_params(("arbitrary", "arbitrary")),
        name="attn_sample",
    )(page_table, q, knt, v_new, gate, subln_g.reshape(1, LANE),
      *[t.reshape(1, ATT_HEAD_DIM) for t in lams], *([cache_kt] * pages), *([cache_v2] * pages))


def _conv_in_kernel(*refs, nb, tb, tiles_per_batch, has_state):
    if has_state:
        (x_ref, ng_ref, wa_ref, wl_ref, wg_ref, cw_ref, cb_ref, st_in_ref,
         y_ref, gate_ref, st_ref, h_s, ext_s, wb_s) = refs
        carry_s = None
    else:
        (x_ref, ng_ref, wa_ref, wl_ref, wg_ref, cw_ref, cb_ref,
         y_ref, gate_ref, st_ref, h_s, ext_s, wb_s, carry_s) = refs
    i = pl.program_id(0)
    cb = pl.program_id(1)

    @pl.when(cb == 0)
    def _():
        _rms_rows(x_ref, ng_ref, h_s)

    a = jnp.dot(h_s[...], wa_ref[...], preferred_element_type=F32)
    glu = jnp.dot(h_s[...], wl_ref[...], preferred_element_type=F32)
    c = a * jax.nn.sigmoid(glu)

    head0 = CONV_HEAD - CONV_STATE
    if has_state:
        ext_s[:, 0:head0, :] = jnp.zeros((nb, head0, COL_BLOCK), F32)
        ext_s[:, head0:CONV_HEAD, :] = st_in_ref[...]
    else:
        first = (i % tiles_per_batch) == 0

        @pl.when(first)
        def _():
            ext_s[0, 0:CONV_HEAD, :] = jnp.zeros((CONV_HEAD, COL_BLOCK), F32)

        @pl.when(jnp.logical_not(first))
        def _():
            ext_s[0, 0:CONV_HEAD, :] = carry_s[cb]

    for b in range(nb):
        ext_s[b, CONV_HEAD:CONV_HEAD + tb, :] = c[b * tb:(b + 1) * tb, :]

    rc = min(tb, CONV_ROWS)
    n_lane = COL_BLOCK // CONV_LANES
    n_row = tb // rc

    for j in range(CONV_WIDTH):
        wb_s[j] = jnp.broadcast_to(cw_ref[j:j + 1, :], (SUBLANES, COL_BLOCK))

    def chunk(it, carry):
        b = it // (n_row * n_lane)
        r0 = pl.multiple_of(((it // n_lane) % n_row) * rc, rc)
        l0 = pl.multiple_of((it % n_lane) * CONV_LANES, CONV_LANES)
        blk = ext_s[b, pl.ds(r0, rc + CONV_HEAD), pl.ds(l0, CONV_LANES)]
        acc = jnp.broadcast_to(cb_ref[:, pl.ds(l0, CONV_LANES)], (rc, CONV_LANES))
        for r in range(SUBLANES):
            shifted = blk if r == 0 else blk[r:r + rc + CONV_HEAD - SUBLANES, :]
            for j in range(CONV_WIDTH):
                if (head0 + j) % SUBLANES == r:
                    a = head0 + j - r
                    w_j = pltpu.repeat(wb_s[j, :, pl.ds(l0, CONV_LANES)], rc // SUBLANES, axis=0)
                    acc = acc + w_j * shifted[a:a + rc, :]
        y_ref[pl.ds(b * tb + r0, rc), pl.ds(l0, CONV_LANES)] = acc
        return carry

    lax.fori_loop(0, nb * n_row * n_lane, chunk, 0, unroll=2)
    col0 = pl.multiple_of(cb * COL_BLOCK, COL_BLOCK)
    if has_state:
        for b in range(nb):
            st_ref[b, :, pl.ds(col0, COL_BLOCK)] = ext_s[b, tb + head0:tb + CONV_HEAD, :]
    else:
        @pl.when((i % tiles_per_batch) == tiles_per_batch - 1)
        def _():
            st_ref[i // tiles_per_batch, :, pl.ds(col0, COL_BLOCK)] = ext_s[0, tb + head0:tb + CONV_HEAD, :]

        carry_s[cb] = ext_s[0, tb:tb + CONV_HEAD, :]

    gate_ref[...] = jnp.dot(h_s[...], wg_ref[...], preferred_element_type=F32).astype(BF16)


def _conv_in(x2d, norm_g, w_in, conv_w, conv_b, state, *, n_batch, seq, tm):
    m = x2d.shape[0]
    has_state = state is not None
    if has_state:
        assert tm == m
        nb, tb, tiles_per_batch = n_batch, seq, 1
    else:
        assert seq % tm == 0 and tm >= CONV_HEAD
        nb, tb, tiles_per_batch = 1, tm, seq // tm
    n_cb = D_INNER // COL_BLOCK
    in_specs = [
        pl.BlockSpec((tm, D_MODEL), lambda i, c: (i, 0)),
        pl.BlockSpec((1, D_MODEL), lambda i, c: (0, 0)),
        pl.BlockSpec((D_MODEL, COL_BLOCK), lambda i, c: (0, c)),
        pl.BlockSpec((D_MODEL, COL_BLOCK), lambda i, c: (0, n_cb + c)),
        pl.BlockSpec((D_MODEL, COL_BLOCK), lambda i, c: (0, 2 * n_cb + c)),
        pl.BlockSpec((CONV_WIDTH, COL_BLOCK), lambda i, c: (0, c)),
        pl.BlockSpec((1, COL_BLOCK), lambda i, c: (0, c)),
    ]
    args = [x2d, norm_g.reshape(1, D_MODEL), w_in, w_in, w_in, conv_w, conv_b.reshape(1, D_INNER)]
    scratch = [
        pltpu.VMEM((tm, D_MODEL), BF16),
        pltpu.VMEM((nb, CONV_HEAD + tb, COL_BLOCK), F32),
        pltpu.VMEM((CONV_WIDTH, SUBLANES, COL_BLOCK), F32),
    ]
    if has_state:
        in_specs.append(pl.BlockSpec((nb, CONV_STATE, COL_BLOCK), lambda i, c: (0, 0, c)))
        args.append(state)
    else:
        scratch.append(pltpu.VMEM((n_cb, CONV_HEAD, COL_BLOCK), F32))
    kern = functools.partial(_conv_in_kernel, nb=nb, tb=tb, tiles_per_batch=tiles_per_batch,
                             has_state=has_state)
    return pl.pallas_call(
        kern,
        grid=(m // tm, n_cb),
        in_specs=in_specs,
        out_specs=[
            pl.BlockSpec((tm, COL_BLOCK), lambda i, c: (i, c)),
            pl.BlockSpec((tm, COL_BLOCK), lambda i, c: (i, c)),
            pl.BlockSpec((n_batch, CONV_STATE, D_INNER), lambda i, c: (0, 0, 0)),
        ],
        out_shape=[
            jax.ShapeDtypeStruct((m, D_INNER), F32),
            jax.ShapeDtypeStruct((m, D_INNER), BF16),
            jax.ShapeDtypeStruct((n_batch, CONV_STATE, D_INNER), F32),
        ],
        scratch_shapes=scratch,
        compiler_params=_params(("arbitrary", "arbitrary")),
        name="conv_in",
    )(*args)


PROMPT_TM = 512
OUT_TM = 1024
LN_TM = 512
LN_TN = 512
ATTN_TQ = 1024
ATTN_TK = 1024
SAMPLE_PAGES_PER_STEP = 8


def kernel(x_prompt, x_sample, state_pool_l0, cache_k_l1, cache_v_l1, state_conv_l2, state_pool_l3, page_table, norm_g_l0, w_in_l0, w_grp_l0, pool_scale_l0, w_out_l0, norm_g_l1, w_in_l1, q_norm_g_l1, k_norm_g_l1, lambda_q1_l1, lambda_k1_l1, lambda_q2_l1, lambda_k2_l1, subln_g_l1, w_out_l1, norm_g_l2, w_in_l2, conv_w_l2, conv_b_l2, ln_g_l2, ln_b_l2, w_out_l2, norm_g_l3, w_in_l3, w_grp_l3, pool_scale_l3, w_out_l3):
    nbp, seq, _ = x_prompt.shape
    nbs, t_new, _ = x_sample.shape
    mp, ms = nbp * seq, nbs * t_new
    xp = x_prompt.reshape(mp, D_MODEL)
    xs = x_sample.reshape(ms, D_MODEL)
    lams = (lambda_q1_l1, lambda_k1_l1, lambda_q2_l1, lambda_k2_l1)

    def bf(w):
        return w.astype(BF16)

    def pool_layer(xp, xs, state, norm_g, w_in, w_grp, scale, w_out):
        w_in, w_grp, w_out = bf(w_in), bf(w_grp), bf(w_out)
        zp, stp = _pool_in(xp, norm_g, w_in, w_grp, scale, None, n_batch=nbp, seq=seq, tm=PROMPT_TM)
        zs, sts = _pool_in(xs, norm_g, w_in, w_grp, scale, state, n_batch=nbs, seq=t_new, tm=ms)
        xp = _out_proj(zp, w_out, xp, tm=OUT_TM, tn=COL_BLOCK)
        xs = _out_proj(zs, w_out, xs, tm=ms, tn=COL_BLOCK)
        return xp, xs, stp, sts

    xp, xs, pool0_p, pool0_s = pool_layer(xp, xs, state_pool_l0, norm_g_l0, w_in_l0, w_grp_l0,
                                          pool_scale_l0, w_out_l0)

    w_in, w_out = bf(w_in_l1), bf(w_out_l1)
    qp, ktp, vp, vnp, gp = _attn_in(xp, norm_g_l1, w_in, q_norm_g_l1, k_norm_g_l1, n_batch=nbp, seq=seq,
                                    tm=PROMPT_TM, q_dtype=BF16, v_dtype=BF16, native_v=True)
    qs, kts, vs, gs = _attn_in(xs, norm_g_l1, w_in, q_norm_g_l1, k_norm_g_l1, n_batch=1, seq=ms,
                               tm=ms, q_dtype=F32, v_dtype=F32, native_v=False)
    shp = (nbp, seq, D_MODEL)
    zp = _attn_prompt(qp.reshape(shp), ktp, vp.reshape(shp), gp.reshape(shp), subln_g_l1, lams,
                      n_batch=nbp, seq=seq, tq=ATTN_TQ, tk=ATTN_TK)
    n_phys = cache_k_l1.shape[0]
    cache_kt = jnp.transpose(cache_k_l1, (0, 2, 3, 4, 1)).reshape(n_phys, D_MODEL, PAGE_SIZE)
    cache_v2 = cache_v_l1.reshape(n_phys, PAGE_SIZE * ATT_HEADS, ATT_VDIM)
    shs = (nbs, t_new, D_MODEL)
    zs = _attn_sample(qs.reshape(shs), kts[0], vs, gs.reshape(shs), cache_kt, cache_v2, page_table,
                      subln_g_l1, lams, pages=SAMPLE_PAGES_PER_STEP)
    xp = _out_proj(zp.reshape(mp, D_MODEL), w_out, xp, tm=OUT_TM, tn=COL_BLOCK)
    xs = _out_proj(zs.reshape(ms, D_MODEL).astype(BF16), w_out, xs, tm=ms, tn=COL_BLOCK)
    new_k_p = jnp.transpose(ktp.reshape(nbp, ATT_HEADS, 2, ATT_HEAD_DIM, seq), (0, 4, 1, 2, 3))
    new_v_p = vnp.reshape(nbp, seq, ATT_HEADS, ATT_VDIM)
    new_k_s = kts[0].T.reshape(nbs, t_new, ATT_HEADS, 2, ATT_HEAD_DIM)
    new_v_s = vs.reshape(nbs, t_new, ATT_HEADS, ATT_VDIM)

    w_in, w_out = bf(w_in_l2), bf(w_out_l2)
    yp, gp, conv_p = _conv_in(xp, norm_g_l2, w_in, conv_w_l2, conv_b_l2, None, n_batch=nbp, seq=seq,
                              tm=PROMPT_TM)
    ys, gs, conv_s = _conv_in(xs, norm_g_l2, w_in, conv_w_l2, conv_b_l2, state_conv_l2, n_batch=nbs,
                              seq=t_new, tm=ms)
    xp = _ln_out_proj(yp, gp, ln_g_l2, ln_b_l2, w_out, xp, tm=LN_TM, tn=LN_TN)
    xs = _ln_out_proj(ys, gs, ln_g_l2, ln_b_l2, w_out, xs, tm=ms, tn=LN_TN)

    xp, xs, pool3_p, pool3_s = pool_layer(xp, xs, state_pool_l3, norm_g_l3, w_in_l3, w_grp_l3,
                                          pool_scale_l3, w_out_l3)

    return (xp.reshape(nbp, seq, D_MODEL), xs.reshape(nbs, t_new, D_MODEL), pool0_p, pool0_s,
            new_k_p, new_v_p, new_k_s, new_v_s, conv_p, conv_s, pool3_p, pool3_s)
```

```python
import functools
import math

import jax
import jax.numpy as jnp
from jax import lax
from jax.experimental import pallas as pl
from jax.experimental.pallas import tpu as pltpu

F32 = jnp.float32
BF16 = jnp.bfloat16

D_MODEL = 2048
D_INNER = 4096
POOL_WINDOWS = (2, 4, 8, 16)
POOL_GROUP = 1024
POOL_STATE = 15
ATT_HEADS = 16
ATT_HEAD_DIM = 64
ATT_VDIM = 128
LAMBDA_INIT = 0.8 - 0.6 * math.exp(-0.3 * 1)
LOG2E = math.log2(math.e)
QK_SCALE = ATT_HEAD_DIM ** -0.5 * LOG2E
CONV_WIDTH = 31
CONV_STATE = 30
NORM_EPS = 1e-6
NEG_INF = -1e30
PAGE_SIZE = 128

VMEM_LIMIT_BYTES = 58 * 1024 * 1024
LANE = 128
SUBLANES = 8
BF16_ROWS = 16
COL_BLOCK = 1024
POOL_HEAD = 16
CONV_HEAD = 32
LANE_CHUNK = 256
CONV_ROWS = 64
CONV_LANES = 128


def _params(sem):
    return pltpu.CompilerParams(dimension_semantics=sem, vmem_limit_bytes=VMEM_LIMIT_BYTES)


def _rms_rows(x_ref, g_ref, h_ref):
    xf = x_ref[...]
    ms = jnp.mean(xf * xf, axis=-1, keepdims=True)
    h_ref[...] = (xf * lax.rsqrt(ms + NORM_EPS) * g_ref[...]).astype(BF16)


def _silu(x):
    return x * jax.nn.sigmoid(x)


def _pool_rows(blk, w, n_seen):
    rows = blk.shape[0] - POOL_HEAD
    cur = blk[POOL_HEAD:, :]
    acc = cur
    for j in range(1, w):
        acc = acc + blk[POOL_HEAD - j:POOL_HEAD - j + rows, :]
    cnt = jnp.minimum(n_seen, w).astype(F32)
    return jnp.where(n_seen >= w, acc * (1.0 / w), acc / cnt) - cur


def _pool_finish(h_s, pooled, wg_ref, wgrp_ref, sc_ref, z_ref):
    mixed = jnp.dot(pooled, wgrp_ref[0], preferred_element_type=F32) * sc_ref[...]
    gate = jnp.dot(h_s[...], wg_ref[...], preferred_element_type=F32)
    z_ref[...] = (mixed * _silu(gate)).astype(BF16)


def _pool_in_prompt_kernel(x_ref, ng_ref, wu_ref, wg_ref, wgrp_ref, sc_ref, pmat_ref,
                           z_ref, st_ref, h_s, edge_s, pooled_s, carry_s, *, tm, tiles_per_batch):
    g = pl.program_id(0)
    i = pl.program_id(1)
    _rms_rows(x_ref, ng_ref, h_s)

    u = jnp.dot(h_s[...], wu_ref[...], preferred_element_type=F32)
    first = (i % tiles_per_batch) == 0

    pooled_s[...] = jnp.dot(pmat_ref[0], u.astype(BF16), preferred_element_type=F32).astype(BF16)

    @pl.when(first)
    def _():
        edge_s[0:POOL_HEAD, :] = jnp.zeros((POOL_HEAD, COL_BLOCK), F32)

    @pl.when(jnp.logical_not(first))
    def _():
        edge_s[0:POOL_HEAD, :] = carry_s[g]

    edge_s[POOL_HEAD:, :] = u[0:POOL_HEAD, :]
    n_seen = ((i % tiles_per_batch) * tm + 1
              + lax.broadcasted_iota(jnp.int32, (POOL_HEAD, LANE_CHUNK), 0))
    for k, w in enumerate(POOL_WINDOWS):
        @pl.when(g == k)
        def _(w=w):
            for l0 in range(0, COL_BLOCK, LANE_CHUNK):
                pooled_s[0:POOL_HEAD, l0:l0 + LANE_CHUNK] = _pool_rows(
                    edge_s[:, l0:l0 + LANE_CHUNK], w, n_seen).astype(BF16)

    @pl.when((i % tiles_per_batch) == tiles_per_batch - 1)
    def _():
        col0 = pl.multiple_of(g * COL_BLOCK, COL_BLOCK)
        st_ref[i // tiles_per_batch, :, pl.ds(col0, COL_BLOCK)] = u[tm - POOL_STATE:tm, :]

    carry_s[g] = u[tm - POOL_HEAD:tm, :]
    _pool_finish(h_s, pooled_s[...], wg_ref, wgrp_ref, sc_ref, z_ref)


def _pool_in_sample_kernel(x_ref, ng_ref, wu_ref, wg_ref, wgrp_ref, sc_ref, st_in_ref,
                           z_ref, st_ref, h_s, ext_s, pooled_s, *, nb, tb):
    g = pl.program_id(1)

    @pl.when(g == 0)
    def _():
        _rms_rows(x_ref, ng_ref, h_s)

    u = jnp.dot(h_s[...], wu_ref[...], preferred_element_type=F32)

    ext_s[:, 0:1, :] = jnp.zeros((nb, 1, COL_BLOCK), F32)
    ext_s[:, 1:POOL_HEAD, :] = st_in_ref[...]
    for b in range(nb):
        ext_s[b, POOL_HEAD:, :] = u[b * tb:(b + 1) * tb, :]

    n_seen = POOL_STATE + 1 + lax.broadcasted_iota(jnp.int32, (tb, LANE_CHUNK), 0)
    for k, w in enumerate(POOL_WINDOWS):
        @pl.when(g == k)
        def _(w=w):
            for b in range(nb):
                for l0 in range(0, COL_BLOCK, LANE_CHUNK):
                    pooled_s[b * tb:(b + 1) * tb, l0:l0 + LANE_CHUNK] = _pool_rows(
                        ext_s[b, :, l0:l0 + LANE_CHUNK], w, n_seen)

    col0 = pl.multiple_of(g * COL_BLOCK, COL_BLOCK)
    for b in range(nb):
        st_ref[b, :, pl.ds(col0, COL_BLOCK)] = ext_s[b, tb + 1:tb + POOL_HEAD, :]
    _pool_finish(h_s, pooled_s[...].astype(BF16), wg_ref, wgrp_ref, sc_ref, z_ref)


def _pool_matrices(tm):
    t = jnp.arange(tm)[:, None]
    j = jnp.arange(tm)[None, :]
    mats = [jnp.where((t - j >= 0) & (t - j < w), 1.0 / w, 0.0) - (t == j) for w in POOL_WINDOWS]
    return jnp.stack(mats).astype(BF16)


def _pool_in(x2d, norm_g, w_in, w_grp, scale, state, *, n_batch, seq, tm):
    m = x2d.shape[0]
    n_groups = len(POOL_WINDOWS)
    in_specs = [
        pl.BlockSpec((tm, D_MODEL), lambda i, g: (i, 0)),
        pl.BlockSpec((1, D_MODEL), lambda i, g: (0, 0)),
        pl.BlockSpec((D_MODEL, COL_BLOCK), lambda i, g: (0, g)),
        pl.BlockSpec((D_MODEL, COL_BLOCK), lambda i, g: (0, n_groups + g)),
        pl.BlockSpec((1, POOL_GROUP, POOL_GROUP), lambda i, g: (g, 0, 0)),
        pl.BlockSpec((1, COL_BLOCK), lambda i, g: (0, g)),
    ]
    args = [x2d, norm_g.reshape(1, D_MODEL), w_in, w_in, w_grp, scale.reshape(1, D_INNER)]
    if state is None:
        assert seq % tm == 0 and tm >= POOL_HEAD
        in_specs.append(pl.BlockSpec((1, tm, tm), lambda i, g: (g, 0, 0)))
        args.append(_pool_matrices(tm))
        scratch = [
            pltpu.VMEM((tm, D_MODEL), BF16),
            pltpu.VMEM((2 * POOL_HEAD, COL_BLOCK), F32),
            pltpu.VMEM((tm, COL_BLOCK), BF16),
            pltpu.VMEM((n_groups, POOL_HEAD, COL_BLOCK), F32),
        ]
        kern = functools.partial(_pool_in_prompt_kernel, tm=tm, tiles_per_batch=seq // tm)
    else:
        assert tm == m
        in_specs.append(pl.BlockSpec((n_batch, POOL_STATE, COL_BLOCK), lambda i, g: (0, 0, g)))
        args.append(state)
        scratch = [
            pltpu.VMEM((tm, D_MODEL), BF16),
            pltpu.VMEM((n_batch, POOL_HEAD + seq, COL_BLOCK), F32),
            pltpu.VMEM((tm, COL_BLOCK), F32),
        ]
        kern = functools.partial(_pool_in_sample_kernel, nb=n_batch, tb=seq)
    out_specs = [
        pl.BlockSpec((tm, COL_BLOCK), lambda i, g: (i, g)),
        pl.BlockSpec((n_batch, POOL_STATE, D_INNER), lambda i, g: (0, 0, 0)),
    ]
    grid = (m // tm, n_groups)
    if state is None:
        def swapped(spec):
            return pl.BlockSpec(spec.block_shape, lambda g, i, f=spec.index_map: f(i, g))

        in_specs = [swapped(s) for s in in_specs]
        out_specs = [swapped(s) for s in out_specs]
        grid = (n_groups, m // tm)
    return pl.pallas_call(
        kern,
        grid=grid,
        in_specs=in_specs,
        out_specs=out_specs,
        out_shape=[
            jax.ShapeDtypeStruct((m, D_INNER), BF16),
            jax.ShapeDtypeStruct((n_batch, POOL_STATE, D_INNER), F32),
        ],
        scratch_shapes=scratch,
        compiler_params=_params(("arbitrary", "arbitrary")),
        name="pool_in",
    )(*args)


def _out_proj_kernel(z_ref, w_ref, x_ref, o_ref):
    o_ref[...] = x_ref[...] + jnp.dot(z_ref[...], w_ref[...], preferred_element_type=F32)


def _out_proj(z, w, x2d, *, tm, tn):
    m, k = z.shape
    n = w.shape[1]
    return pl.pallas_call(
        _out_proj_kernel,
        grid=(m // tm, n // tn),
        in_specs=[
            pl.BlockSpec((tm, k), lambda i, j: (i, 0)),
            pl.BlockSpec((k, tn), lambda i, j: (0, j)),
            pl.BlockSpec((tm, tn), lambda i, j: (i, j)),
        ],
        out_specs=pl.BlockSpec((tm, tn), lambda i, j: (i, j)),
        out_shape=jax.ShapeDtypeStruct((m, n), F32),
        compiler_params=_params(("arbitrary", "arbitrary")),
        name="out_proj",
    )(z, w, x2d)


def _ln_out_proj_kernel(y_ref, gate_ref, lg_ref, lb_ref, w_ref, x_ref, o_ref, z_s):
    @pl.when(pl.program_id(1) == 0)
    def _():
        def rows(it, carry):
            r0 = pl.multiple_of(it * BF16_ROWS, BF16_ROWS)
            y = y_ref[pl.ds(r0, BF16_ROWS), :]
            yc = y - jnp.mean(y, axis=-1, keepdims=True)
            var = jnp.mean(yc * yc, axis=-1, keepdims=True)
            ln = yc * lax.rsqrt(var + NORM_EPS) * lg_ref[...] + lb_ref[...]
            gate = gate_ref[pl.ds(r0, BF16_ROWS), :].astype(F32)
            z_s[pl.ds(r0, BF16_ROWS), :] = (_silu(ln) * _silu(gate)).astype(BF16)
            return carry

        lax.fori_loop(0, z_s.shape[0] // BF16_ROWS, rows, 0, unroll=4)

    o_ref[...] = x_ref[...] + jnp.dot(z_s[...], w_ref[...], preferred_element_type=F32)


def _ln_out_proj(y, gate, ln_g, ln_b, w, x2d, *, tm, tn):
    m, k = y.shape
    n = w.shape[1]
    return pl.pallas_call(
        _ln_out_proj_kernel,
        grid=(m // tm, n // tn),
        in_specs=[
            pl.BlockSpec((tm, k), lambda i, j: (i, 0)),
            pl.BlockSpec((tm, k), lambda i, j: (i, 0)),
            pl.BlockSpec((1, k), lambda i, j: (0, 0)),
            pl.BlockSpec((1, k), lambda i, j: (0, 0)),
            pl.BlockSpec((k, tn), lambda i, j: (0, j)),
            pl.BlockSpec((tm, tn), lambda i, j: (i, j)),
        ],
        out_specs=pl.BlockSpec((tm, tn), lambda i, j: (i, j)),
        out_shape=jax.ShapeDtypeStruct((m, n), F32),
        scratch_shapes=[pltpu.VMEM((tm, k), BF16)],
        compiler_params=_params(("arbitrary", "arbitrary")),
        name="ln_out_proj",
    )(y, gate, ln_g.reshape(1, k), ln_b.reshape(1, k), w, x2d)


def _head_norm(y, g128):
    lane = lax.broadcasted_iota(jnp.int32, (1, LANE), 1)
    low = lane < ATT_HEAD_DIM
    outs = []
    for c in range(y.shape[1] // LANE):
        blk = y[:, c * LANE:(c + 1) * LANE]
        sq = blk * blk
        s_lo = jnp.sum(jnp.where(low, sq, 0.0), axis=-1, keepdims=True)
        s_hi = jnp.sum(jnp.where(low, 0.0, sq), axis=-1, keepdims=True)
        ms = jnp.where(low, s_lo, s_hi) * (1.0 / ATT_HEAD_DIM)
        outs.append(blk * lax.rsqrt(ms + NORM_EPS) * g128)
    return jnp.concatenate(outs, axis=1)


def _attn_in_kernel(x_ref, ng_ref, w_ref, qg_ref, kg_ref, q_ref, kt_ref, v_ref, vn_ref, gate_ref, h_s,
                    *, q_dtype, v_dtype, tm):
    j = pl.program_id(1)

    @pl.when(j == 0)
    def _():
        _rms_rows(x_ref, ng_ref, h_s)

    @pl.when(j < 2)
    def _():
        y = jnp.dot(h_s[...], w_ref[...], preferred_element_type=F32)
        q_ref[...] = (_head_norm(y, qg_ref[...]) * QK_SCALE).astype(q_dtype)

    @pl.when(jnp.logical_and(j >= 2, j < 4))
    def _():
        yt = lax.dot_general(w_ref[...], h_s[...], (((0,), (1,)), ((), ())), preferred_element_type=F32)
        y3 = yt.reshape(COL_BLOCK // ATT_HEAD_DIM, ATT_HEAD_DIM, tm)
        ms = jnp.mean(y3 * y3, axis=1, keepdims=True)
        kt_ref[...] = (y3 * lax.rsqrt(ms + NORM_EPS) * kg_ref[...][None]).reshape(COL_BLOCK, tm)

    for jj in (4, 5):
        @pl.when(j == jj)
        def _(jj=jj):
            y = jnp.dot(h_s[...], w_ref[...], preferred_element_type=F32)
            v_ref[...] = y.astype(v_dtype)
            if vn_ref is not None:
                for hl in range(COL_BLOCK // LANE):
                    hh = (jj - 4) * (COL_BLOCK // LANE) + hl
                    vn_ref[pl.ds(hh, tm, stride=ATT_HEADS), :] = y[:, hl * LANE:(hl + 1) * LANE]

    @pl.when(j >= 6)
    def _():
        gate_ref[...] = jnp.dot(h_s[...], w_ref[...], preferred_element_type=F32).astype(gate_ref.dtype)


def _attn_in(x2d, norm_g, w_in, q_norm_g, k_norm_g, *, n_batch, seq, tm, q_dtype, v_dtype, native_v):
    m = x2d.shape[0]
    tiles_per_batch = seq // tm
    n_col = w_in.shape[1] // COL_BLOCK

    def out_spec(first):
        return pl.BlockSpec((tm, COL_BLOCK), lambda i, j: (i, jnp.clip(j - first, 0, 1)))

    out_specs = [
        out_spec(0),
        pl.BlockSpec((None, COL_BLOCK, tm),
                     lambda i, j: (i // tiles_per_batch, jnp.clip(j - 2, 0, 1), i % tiles_per_batch)),
        out_spec(4),
    ]
    out_shape = [
        jax.ShapeDtypeStruct((m, D_MODEL), q_dtype),
        jax.ShapeDtypeStruct((n_batch, D_MODEL, seq), F32),
        jax.ShapeDtypeStruct((m, D_MODEL), v_dtype),
    ]
    if native_v:
        out_specs.append(pl.BlockSpec((tm * ATT_HEADS, LANE), lambda i, j: (i, 0)))
        out_shape.append(jax.ShapeDtypeStruct((m * ATT_HEADS, LANE), F32))
    out_specs.append(out_spec(6))
    out_shape.append(jax.ShapeDtypeStruct((m, D_MODEL), v_dtype))

    def kern(*refs):
        ins, outs, scratch = refs[:5], refs[5:-1], refs[-1]
        if native_v:
            q_ref, kt_ref, v_ref, vn_ref, gate_ref = outs
        else:
            (q_ref, kt_ref, v_ref, gate_ref), vn_ref = outs, None
        _attn_in_kernel(*ins, q_ref, kt_ref, v_ref, vn_ref, gate_ref, scratch,
                        q_dtype=q_dtype, v_dtype=v_dtype, tm=tm)

    return pl.pallas_call(
        kern,
        grid=(m // tm, n_col),
        in_specs=[
            pl.BlockSpec((tm, D_MODEL), lambda i, j: (i, 0)),
            pl.BlockSpec((1, D_MODEL), lambda i, j: (0, 0)),
            pl.BlockSpec((D_MODEL, COL_BLOCK), lambda i, j: (0, j)),
            pl.BlockSpec((1, LANE), lambda i, j: (0, 0)),
            pl.BlockSpec((ATT_HEAD_DIM, 1), lambda i, j: (0, 0)),
        ],
        out_specs=out_specs,
        out_shape=out_shape,
        scratch_shapes=[pltpu.VMEM((tm, D_MODEL), BF16)],
        compiler_params=_params(("arbitrary", "arbitrary")),
        name="attn_in",
    )(x2d, norm_g.reshape(1, D_MODEL), w_in, jnp.tile(q_norm_g, 2).reshape(1, LANE),
      k_norm_g.reshape(ATT_HEAD_DIM, 1))


def _lambda(lq1_ref, lk1_ref, lq2_ref, lk2_ref):
    a = jnp.sum(lq1_ref[...] * lk1_ref[...], axis=-1, keepdims=True)
    b = jnp.sum(lq2_ref[...] * lk2_ref[...], axis=-1, keepdims=True)
    return jnp.exp(a) - jnp.exp(b) + LAMBDA_INIT


def _sub_ln_gate(o, subln_g, gate):
    ms = jnp.mean(o * o, axis=-1, keepdims=True)
    o = o * lax.rsqrt(ms + NORM_EPS) * subln_g * (1.0 - LAMBDA_INIT)
    return o * _silu(gate)


def _attn_prompt_kernel(q_ref, kt_ref, v_ref, gate_ref, sg_ref, lq1_ref, lk1_ref, lq2_ref, lk2_ref,
                        z_ref, ka_s, m_s, l_s, acc_s, *, tq, tk):
    h = pl.program_id(1)
    qi = pl.program_id(2)
    seq = kt_ref.shape[1]
    slope2 = jnp.exp2(-0.5 * jnp.full((1, LANE), h + 1, jnp.int32).astype(F32)) * LOG2E

    @pl.when(qi == 0)
    def _():
        col = lax.broadcasted_iota(jnp.int32, (BF16_ROWS, seq), 1) % tk
        row = lax.broadcasted_iota(jnp.int32, (BF16_ROWS, seq), 0)
        b = pltpu.repeat(slope2, seq // LANE, axis=1) * col.astype(F32)
        hi = b.astype(BF16).astype(F32)
        mid = (b - hi).astype(BF16).astype(F32)
        lo = b - hi - mid
        rows = jnp.where(row == 0, hi, jnp.where(row == 1, mid, jnp.where(row == 2, lo, 0.0))).astype(BF16)
        zeros = jnp.zeros((ATT_HEAD_DIM - BF16_ROWS, seq), BF16)
        ka_s[0, 0:ATT_HEAD_DIM, :] = kt_ref[0:ATT_HEAD_DIM, :].astype(BF16)
        ka_s[0, ATT_HEAD_DIM:ATT_HEAD_DIM + BF16_ROWS, :] = rows
        ka_s[0, ATT_HEAD_DIM + BF16_ROWS:, :] = zeros
        ka_s[1, 0:BF16_ROWS, :] = rows
        ka_s[1, BF16_ROWS:ATT_HEAD_DIM, :] = zeros
        ka_s[1, ATT_HEAD_DIM:, :] = kt_ref[ATT_HEAD_DIM:, :].astype(BF16)

    q = q_ref[...]
    lane = lax.broadcasted_iota(jnp.int32, (1, LANE), 1)
    low = lane < ATT_HEAD_DIM
    zero = jnp.zeros_like(q)
    ones1 = jnp.where(jnp.logical_and(lane >= ATT_HEAD_DIM, lane < ATT_HEAD_DIM + 3), 1.0, 0.0).astype(BF16)
    ones2 = jnp.where(lane < 3, 1.0, 0.0).astype(BF16)
    qa = (jnp.where(low, q, zero) + ones1, jnp.where(low, zero, q) + ones2)

    m_s[...] = jnp.full(m_s.shape, NEG_INF, F32)
    l_s[...] = jnp.zeros(l_s.shape, F32)
    acc_s[...] = jnp.zeros(acc_s.shape, F32)

    def tile(kj, masked, r0=0, r1=tq, n_col=tk):
        start = pl.multiple_of(kj * tk, tk)
        vt = v_ref[pl.ds(start, n_col), :]
        shift = slope2 * jnp.full((1, LANE), kj * tk - qi * tq, jnp.int32).astype(F32)
        for c in range(2):
            s = jnp.dot(qa[c][r0:r1], ka_s[c, :, pl.ds(start, n_col)], preferred_element_type=F32)
            if masked:
                row_i = lax.broadcasted_iota(jnp.int32, (r1 - r0, n_col), 0) + r0
                col_j = lax.broadcasted_iota(jnp.int32, (r1 - r0, n_col), 1) + (kj * tk - qi * tq)
                s = jnp.where(col_j <= row_i, s, NEG_INF)
            m_old = m_s[c, r0:r1]
            m_new = jnp.maximum(m_old, jnp.max(s, axis=-1, keepdims=True) + shift)
            alpha = jnp.exp2(m_old - m_new)
            p = jnp.exp2(s - pltpu.repeat(m_new - shift, n_col // LANE, axis=1))
            l_s[c, r0:r1] = alpha * l_s[c, r0:r1] + jnp.sum(p, axis=-1, keepdims=True)
            acc_s[c, r0:r1] = alpha * acc_s[c, r0:r1] + jnp.dot(p.astype(BF16), vt, preferred_element_type=F32)
            m_s[c, r0:r1] = m_new

    n_full = (qi * tq) // tk

    def body(kj, carry):
        tile(kj, False)
        return carry

    lax.fori_loop(0, n_full, body, 0)
    if tq == tk:
        tile(n_full, True, 0, tq // 2, tk // 2)
        tile(n_full, True, tq // 2, tq, tk)
    else:
        for d in range(tq // tk):
            tile(n_full + d, True)

    lam = _lambda(lq1_ref, lk1_ref, lq2_ref, lk2_ref)
    o = acc_s[0] / l_s[0] - lam * (acc_s[1] / l_s[1])
    z_ref[...] = _sub_ln_gate(o, sg_ref[...], gate_ref[...].astype(F32)).astype(BF16)


def _attn_prompt(q, kt, v, gate, subln_g, lams, *, n_batch, seq, tq, tk):
    assert tq % tk == 0 and seq % tq == 0
    vec = pl.BlockSpec((1, ATT_HEAD_DIM), lambda b, h, i: (0, 0))
    kern = functools.partial(_attn_prompt_kernel, tq=tq, tk=tk)
    return pl.pallas_call(
        kern,
        grid=(n_batch, ATT_HEADS, seq // tq),
        in_specs=[
            pl.BlockSpec((None, tq, LANE), lambda b, h, i: (b, i, h)),
            pl.BlockSpec((None, LANE, seq), lambda b, h, i: (b, h, 0)),
            pl.BlockSpec((None, seq, LANE), lambda b, h, i: (b, 0, h)),
            pl.BlockSpec((None, tq, LANE), lambda b, h, i: (b, i, h)),
            pl.BlockSpec((1, LANE), lambda b, h, i: (0, 0)),
            vec, vec, vec, vec,
        ],
        out_specs=pl.BlockSpec((None, tq, LANE), lambda b, h, i: (b, i, h)),
        out_shape=jax.ShapeDtypeStruct((n_batch, seq, D_MODEL), BF16),
        scratch_shapes=[
            pltpu.VMEM((2, LANE, seq), BF16),
            pltpu.VMEM((2, tq, LANE), F32),
            pltpu.VMEM((2, tq, LANE), F32),
            pltpu.VMEM((2, tq, LANE), F32),
        ],
        compiler_params=_params(("arbitrary", "arbitrary", "arbitrary")),
        name="attn_prompt",
    )(q, kt, v, gate, subln_g.reshape(1, LANE), *[t.reshape(1, ATT_HEAD_DIM) for t in lams])


def _attn_sample_kernel(pt_ref, q_ref, knt_ref, vn_ref, gate_ref, sg_ref,
                        lq1_ref, lk1_ref, lq2_ref, lk2_ref, *rest, pages, n_steps, past_len, t_new):
    k_refs = rest[:pages]
    v_refs = rest[pages:2 * pages]
    z_ref, q3_s, vh_s, slope_s, m_s, l_s, acc_s = rest[2 * pages:]
    b = pl.program_id(0)
    p = pl.program_id(1)
    n_rows = 2 * t_new
    shape3 = (ATT_HEADS, n_rows, LANE)

    @pl.when(p == 0)
    def _():
        row_c = lax.broadcasted_iota(jnp.int32, (n_rows, LANE), 0) // t_new
        lane_c = lax.broadcasted_iota(jnp.int32, (n_rows, LANE), 1) // ATT_HEAD_DIM
        for hh in range(ATT_HEADS):
            blk = q_ref[:, hh * LANE:(hh + 1) * LANE]
            both = jnp.concatenate([blk, blk], axis=0)
            q3_s[hh] = jnp.where(row_c == lane_c, both, 0.0).astype(BF16)
        head = lax.broadcasted_iota(jnp.int32, shape3, 0)
        slope_s[...] = jnp.exp2(-0.5 * (head + 1).astype(F32)) * LOG2E
        m_s[...] = jnp.full(shape3, NEG_INF, F32)
        l_s[...] = jnp.zeros(shape3, F32)
        acc_s[...] = jnp.zeros(shape3, F32)

    q3 = q3_s[...]

    def update(s, vb):
        m_old = m_s[...]
        m_new = jnp.maximum(m_old, jnp.max(s, axis=-1, keepdims=True))
        alpha = jnp.exp2(m_old - m_new)
        t = s.shape[-1]
        m_all = pltpu.repeat(m_new, t // LANE, axis=2) if t % LANE == 0 else m_new[:, :, 0:t]
        pr = jnp.exp2(s - m_all)
        l_s[...] = alpha * l_s[...] + jnp.sum(pr, axis=-1, keepdims=True)
        pv = jnp.einsum("hrt,hte->hre", pr.astype(BF16), vb, preferred_element_type=F32)
        acc_s[...] = alpha * acc_s[...] + pv
        m_s[...] = m_new

    tok = lax.broadcasted_iota(jnp.int32, (1, 1, PAGE_SIZE), 2)
    slope = slope_s[...]
    scores = []
    for r in range(pages):
        k3 = k_refs[r][...].reshape(ATT_HEADS, LANE, PAGE_SIZE).astype(BF16)
        k_pos = ((p * pages + r) * PAGE_SIZE - past_len + tok).astype(F32)
        scores.append(jnp.einsum("hrk,hkt->hrt", q3, k3, preferred_element_type=F32) + slope * k_pos)
        v3 = v_refs[r][...].reshape(PAGE_SIZE, ATT_HEADS, LANE)
        vh_s[:, r * PAGE_SIZE:(r + 1) * PAGE_SIZE, :] = jnp.swapaxes(v3, 0, 1).astype(BF16)
    update(jnp.concatenate(scores, axis=-1), vh_s[...])

    @pl.when(p == n_steps - 1)
    def _():
        n_tok = knt_ref.shape[1]
        k3 = knt_ref[...].reshape(ATT_HEADS, LANE, n_tok).astype(BF16)
        tok_n = lax.broadcasted_iota(jnp.int32, (1, n_rows, n_tok), 2)
        row_t = lax.broadcasted_iota(jnp.int32, (1, n_rows, n_tok), 1) % t_new
        valid = jnp.logical_and(tok_n // t_new == b, tok_n % t_new <= row_t)
        s = jnp.einsum("hrk,hkt->hrt", q3, k3, preferred_element_type=F32)
        s = s + slope[:, :, 0:n_tok] * (tok_n % t_new).astype(F32)
        s = jnp.where(valid, s, NEG_INF)
        vb = jnp.stack([vn_ref[:, hh * LANE:(hh + 1) * LANE] for hh in range(ATT_HEADS)], axis=0).astype(BF16)
        update(s, vb)

        lam = _lambda(lq1_ref, lk1_ref, lq2_ref, lk2_ref)
        acc = acc_s[...]
        l = l_s[...]
        o = acc[:, 0:t_new, :] / l[:, 0:t_new, :] - lam * (acc[:, t_new:, :] / l[:, t_new:, :])
        for hh in range(ATT_HEADS):
            gate = gate_ref[:, hh * LANE:(hh + 1) * LANE]
            z_ref[:, hh * LANE:(hh + 1) * LANE] = _sub_ln_gate(o[hh], sg_ref[...], gate)


def _attn_sample(q, knt, v_new, gate, cache_kt, cache_v2, page_table, subln_g, lams, *, pages):
    n_batch, t_new, _ = q.shape
    n_pages = page_table.shape[1]
    n_steps = n_pages // pages
    n_rows = 2 * t_new
    n_tok = n_batch * t_new

    tok_spec = pl.BlockSpec((None, t_new, D_MODEL), lambda b, p, pt: (b, 0, 0))
    vec = pl.BlockSpec((1, ATT_HEAD_DIM), lambda b, p, pt: (0, 0))

    def page_spec(r):
        return pl.BlockSpec((None, D_MODEL, PAGE_SIZE), lambda b, p, pt: (pt[b, p * pages + r], 0, 0))

    kern = functools.partial(_attn_sample_kernel, pages=pages, n_steps=n_steps,
                             past_len=n_pages * PAGE_SIZE, t_new=t_new)
    grid_spec = pltpu.PrefetchScalarGridSpec(
        num_scalar_prefetch=1,
        grid=(n_batch, n_steps),
        in_specs=[tok_spec,
                  pl.BlockSpec((D_MODEL, n_tok), lambda b, p, pt: (0, 0)),
                  pl.BlockSpec((n_tok, D_MODEL), lambda b, p, pt: (0, 0)),
                  tok_spec,
                  pl.BlockSpec((1, LANE), lambda b, p, pt: (0, 0)), vec, vec, vec, vec]
        + [page_spec(r) for r in range(pages)] + [page_spec(r) for r in range(pages)],
        out_specs=tok_spec,
        scratch_shapes=[
            pltpu.VMEM((ATT_HEADS, n_rows, LANE), BF16),
            pltpu.VMEM((ATT_HEADS, pages * PAGE_SIZE, LANE), BF16),
            pltpu.VMEM((ATT_HEADS, n_rows, LANE), F32),
            pltpu.VMEM((ATT_HEADS, n_rows, LANE), F32),
            pltpu.VMEM((ATT_HEADS, n_rows, LANE), F32),
            pltpu.VMEM((ATT_HEADS, n_rows, LANE), F32),
        ],
    )
    return pl.pallas_call(
        kern,
        grid_spec=grid_spec,
        out_shape=jax.ShapeDtypeStruct((n_batch, t_new, D_MODEL), F32),
        compiler_params=_params(("arbitrary", "arbitrary")),
        name="attn_sample",
    )(page_table, q, knt, v_new, gate, subln_g.reshape(1, LANE),
      *[t.reshape(1, ATT_HEAD_DIM) for t in lams], *([cache_kt] * pages), *([cache_v2] * pages))


def _conv_in_kernel(*refs, nb, tb, tiles_per_batch, has_state):
    if has_state:
        (x_ref, ng_ref, wa_ref, wl_ref, wg_ref, cw_ref, cb_ref, st_in_ref,
         y_ref, gate_ref, st_ref, h_s, ext_s, wb_s) = refs
        carry_s = None
    else:
        (x_ref, ng_ref, wa_ref, wl_ref, wg_ref, cw_ref, cb_ref,
         y_ref, gate_ref, st_ref, h_s, ext_s, wb_s, carry_s) = refs
    i = pl.program_id(0)
    cb = pl.program_id(1)

    @pl.when(cb == 0)
    def _():
        _rms_rows(x_ref, ng_ref, h_s)

    a = jnp.dot(h_s[...], wa_ref[...], preferred_element_type=F32)
    glu = jnp.dot(h_s[...], wl_ref[...], preferred_element_type=F32)
    c = a * jax.nn.sigmoid(glu)

    head0 = CONV_HEAD - CONV_STATE
    if has_state:
        ext_s[:, 0:head0, :] = jnp.zeros((nb, head0, COL_BLOCK), F32)
        ext_s[:, head0:CONV_HEAD, :] = st_in_ref[...]
    else:
        first = (i % tiles_per_batch) == 0

        @pl.when(first)
        def _():
            ext_s[0, 0:CONV_HEAD, :] = jnp.zeros((CONV_HEAD, COL_BLOCK), F32)

        @pl.when(jnp.logical_not(first))
        def _():
            ext_s[0, 0:CONV_HEAD, :] = carry_s[cb]

    for b in range(nb):
        ext_s[b, CONV_HEAD:CONV_HEAD + tb, :] = c[b * tb:(b + 1) * tb, :]

    rc = min(tb, CONV_ROWS)
    n_lane = COL_BLOCK // CONV_LANES
    n_row = tb // rc

    for j in range(CONV_WIDTH):
        wb_s[j] = jnp.broadcast_to(cw_ref[j:j + 1, :], (SUBLANES, COL_BLOCK))

    def chunk(it, carry):
        b = it // (n_row * n_lane)
        r0 = pl.multiple_of(((it // n_lane) % n_row) * rc, rc)
        l0 = pl.multiple_of((it % n_lane) * CONV_LANES, CONV_LANES)
        blk = ext_s[b, pl.ds(r0, rc + CONV_HEAD), pl.ds(l0, CONV_LANES)]
        acc = jnp.broadcast_to(cb_ref[:, pl.ds(l0, CONV_LANES)], (rc, CONV_LANES))
        for r in range(SUBLANES):
            shifted = blk if r == 0 else blk[r:r + rc + CONV_HEAD - SUBLANES, :]
            for j in range(CONV_WIDTH):
                if (head0 + j) % SUBLANES == r:
                    a = head0 + j - r
                    w_j = pltpu.repeat(wb_s[j, :, pl.ds(l0, CONV_LANES)], rc // SUBLANES, axis=0)
                    acc = acc + w_j * shifted[a:a + rc, :]
        y_ref[pl.ds(b * tb + r0, rc), pl.ds(l0, CONV_LANES)] = acc
        return carry

    lax.fori_loop(0, nb * n_row * n_lane, chunk, 0, unroll=2)
    col0 = pl.multiple_of(cb * COL_BLOCK, COL_BLOCK)
    if has_state:
        for b in range(nb):
            st_ref[b, :, pl.ds(col0, COL_BLOCK)] = ext_s[b, tb + head0:tb + CONV_HEAD, :]
    else:
        @pl.when((i % tiles_per_batch) == tiles_per_batch - 1)
        def _():
            st_ref[i // tiles_per_batch, :, pl.ds(col0, COL_BLOCK)] = ext_s[0, tb + head0:tb + CONV_HEAD, :]

        carry_s[cb] = ext_s[0, tb:tb + CONV_HEAD, :]

    gate_ref[...] = jnp.dot(h_s[...], wg_ref[...], preferred_element_type=F32).astype(BF16)


def _conv_in(x2d, norm_g, w_in, conv_w, conv_b, state, *, n_batch, seq, tm):
    m = x2d.shape[0]
    has_state = state is not None
    if has_state:
        assert tm == m
        nb, tb, tiles_per_batch = n_batch, seq, 1
    else:
        assert seq % tm == 0 and tm >= CONV_HEAD
        nb, tb, tiles_per_batch = 1, tm, seq // tm
    n_cb = D_INNER // COL_BLOCK
    in_specs = [
        pl.BlockSpec((tm, D_MODEL), lambda i, c: (i, 0)),
        pl.BlockSpec((1, D_MODEL), lambda i, c: (0, 0)),
        pl.BlockSpec((D_MODEL, COL_BLOCK), lambda i, c: (0, c)),
        pl.BlockSpec((D_MODEL, COL_BLOCK), lambda i, c: (0, n_cb + c)),
        pl.BlockSpec((D_MODEL, COL_BLOCK), lambda i, c: (0, 2 * n_cb + c)),
        pl.BlockSpec((CONV_WIDTH, COL_BLOCK), lambda i, c: (0, c)),
        pl.BlockSpec((1, COL_BLOCK), lambda i, c: (0, c)),
    ]
    args = [x2d, norm_g.reshape(1, D_MODEL), w_in, w_in, w_in, conv_w, conv_b.reshape(1, D_INNER)]
    scratch = [
        pltpu.VMEM((tm, D_MODEL), BF16),
        pltpu.VMEM((nb, CONV_HEAD + tb, COL_BLOCK), F32),
        pltpu.VMEM((CONV_WIDTH, SUBLANES, COL_BLOCK), F32),
    ]
    if has_state:
        in_specs.append(pl.BlockSpec((nb, CONV_STATE, COL_BLOCK), lambda i, c: (0, 0, c)))
        args.append(state)
    else:
        scratch.append(pltpu.VMEM((n_cb, CONV_HEAD, COL_BLOCK), F32))
    kern = functools.partial(_conv_in_kernel, nb=nb, tb=tb, tiles_per_batch=tiles_per_batch,
                             has_state=has_state)
    return pl.pallas_call(
        kern,
        grid=(m // tm, n_cb),
        in_specs=in_specs,
        out_specs=[
            pl.BlockSpec((tm, COL_BLOCK), lambda i, c: (i, c)),
            pl.BlockSpec((tm, COL_BLOCK), lambda i, c: (i, c)),
            pl.BlockSpec((n_batch, CONV_STATE, D_INNER), lambda i, c: (0, 0, 0)),
        ],
        out_shape=[
            jax.ShapeDtypeStruct((m, D_INNER), F32),
            jax.ShapeDtypeStruct((m, D_INNER), BF16),
            jax.ShapeDtypeStruct((n_batch, CONV_STATE, D_INNER), F32),
        ],
        scratch_shapes=scratch,
        compiler_params=_params(("arbitrary", "arbitrary")),
        name="conv_in",
    )(*args)


PROMPT_TM = 512
OUT_TM = 1024
LN_TM = 512
LN_TN = 512
ATTN_TQ = 1024
ATTN_TK = 1024
SAMPLE_PAGES_PER_STEP = 8


def kernel(x_prompt, x_sample, state_pool_l0, cache_k_l1, cache_v_l1, state_conv_l2, state_pool_l3, page_table, norm_g_l0, w_in_l0, w_grp_l0, pool_scale_l0, w_out_l0, norm_g_l1, w_in_l1, q_norm_g_l1, k_norm_g_l1, lambda_q1_l1, lambda_k1_l1, lambda_q2_l1, lambda_k2_l1, subln_g_l1, w_out_l1, norm_g_l2, w_in_l2, conv_w_l2, conv_b_l2, ln_g_l2, ln_b_l2, w_out_l2, norm_g_l3, w_in_l3, w_grp_l3, pool_scale_l3, w_out_l3):
    nbp, seq, _ = x_prompt.shape
    nbs, t_new, _ = x_sample.shape
    mp, ms = nbp * seq, nbs * t_new
    xp = x_prompt.reshape(mp, D_MODEL)
    xs = x_sample.reshape(ms, D_MODEL)
    lams = (lambda_q1_l1, lambda_k1_l1, lambda_q2_l1, lambda_k2_l1)

    def bf(w):
        return w.astype(BF16)

    def pool_layer(xp, xs, state, norm_g, w_in, w_grp, scale, w_out):
        w_in, w_grp, w_out = bf(w_in), bf(w_grp), bf(w_out)
        zp, stp = _pool_in(xp, norm_g, w_in, w_grp, scale, None, n_batch=nbp, seq=seq, tm=PROMPT_TM)
        zs, sts = _pool_in(xs, norm_g, w_in, w_grp, scale, state, n_batch=nbs, seq=t_new, tm=ms)
        xp = _out_proj(zp, w_out, xp, tm=OUT_TM, tn=COL_BLOCK)
        xs = _out_proj(zs, w_out, xs, tm=ms, tn=COL_BLOCK)
        return xp, xs, stp, sts

    xp, xs, pool0_p, pool0_s = pool_layer(xp, xs, state_pool_l0, norm_g_l0, w_in_l0, w_grp_l0,
                                          pool_scale_l0, w_out_l0)

    w_in, w_out = bf(w_in_l1), bf(w_out_l1)
    qp, ktp, vp, vnp, gp = _attn_in(xp, norm_g_l1, w_in, q_norm_g_l1, k_norm_g_l1, n_batch=nbp, seq=seq,
                                    tm=PROMPT_TM, q_dtype=BF16, v_dtype=BF16, native_v=True)
    qs, kts, vs, gs = _attn_in(xs, norm_g_l1, w_in, q_norm_g_l1, k_norm_g_l1, n_batch=1, seq=ms,
                               tm=ms, q_dtype=F32, v_dtype=F32, native_v=False)
    shp = (nbp, seq, D_MODEL)
    zp = _attn_prompt(qp.reshape(shp), ktp, vp.reshape(shp), gp.reshape(shp), subln_g_l1, lams,
                      n_batch=nbp, seq=seq, tq=ATTN_TQ, tk=ATTN_TK)
    n_phys = cache_k_l1.shape[0]
    cache_kt = jnp.transpose(cache_k_l1, (0, 2, 3, 4, 1)).reshape(n_phys, D_MODEL, PAGE_SIZE)
    cache_v2 = cache_v_l1.reshape(n_phys, PAGE_SIZE * ATT_HEADS, ATT_VDIM)
    shs = (nbs, t_new, D_MODEL)
    zs = _attn_sample(qs.reshape(shs), kts[0], vs, gs.reshape(shs), cache_kt, cache_v2, page_table,
                      subln_g_l1, lams, pages=SAMPLE_PAGES_PER_STEP)
    xp = _out_proj(zp.reshape(mp, D_MODEL), w_out, xp, tm=OUT_TM, tn=COL_BLOCK)
    xs = _out_proj(zs.reshape(ms, D_MODEL).astype(BF16), w_out, xs, tm=ms, tn=COL_BLOCK)
    new_k_p = jnp.transpose(ktp.reshape(nbp, ATT_HEADS, 2, ATT_HEAD_DIM, seq), (0, 4, 1, 2, 3))
    new_v_p = vnp.reshape(nbp, seq, ATT_HEADS, ATT_VDIM)
    new_k_s = kts[0].T.reshape(nbs, t_new, ATT_HEADS, 2, ATT_HEAD_DIM)
    new_v_s = vs.reshape(nbs, t_new, ATT_HEADS, ATT_VDIM)

    w_in, w_out = bf(w_in_l2), bf(w_out_l2)
    yp, gp, conv_p = _conv_in(xp, norm_g_l2, w_in, conv_w_l2, conv_b_l2, None, n_batch=nbp, seq=seq,
                              tm=PROMPT_TM)
    ys, gs, conv_s = _conv_in(xs, norm_g_l2, w_in, conv_w_l2, conv_b_l2, state_conv_l2, n_batch=nbs,
                              seq=t_new, tm=ms)
    xp = _ln_out_proj(yp, gp, ln_g_l2, ln_b_l2, w_out, xp, tm=LN_TM, tn=LN_TN)
    xs = _ln_out_proj(ys, gs, ln_g_l2, ln_b_l2, w_out, xs, tm=ms, tn=LN_TN)

    xp, xs, pool3_p, pool3_s = pool_layer(xp, xs, state_pool_l3, norm_g_l3, w_in_l3, w_grp_l3,
                                          pool_scale_l3, w_out_l3)

    return (xp.reshape(nbp, seq, D_MODEL), xs.reshape(nbs, t_new, D_MODEL), pool0_p, pool0_s,
            new_k_p, new_v_p, new_k_s, new_v_s, conv_p, conv_s, pool3_p, pool3_s)
```
